```python
import functools
import jax
import jax.numpy as jnp
from jax import lax
import numpy as np

D_MODEL = 2048
BATCH = 2
SEQ = 4096
DEPTH = 1
DEC_BATCH = 128
DEC_SEQ = 1
PAST_LEN = 16384
PAGE_SIZE = 128

GLA_HEADS = 4
GLA_DK = 128
GLA_DV = 256
GLA_LOWRANK = 16
GLA_TAU = 16.0
GLA_CHUNK = 64
MLA_HEADS = 8
MLA_Q_RANK = 512
MLA_KV_RANK = 512
MLA_NOPE = 128
MLA_ROPE = 64
MLA_DV = 128
MLA_SCALE = (MLA_NOPE + MLA_ROPE) ** -0.5
ROPE_THETA = 10000.0
Q_BLOCK = 128
PEER_HEADS = 8
PEER_NKEYS = 128
PEER_NEXPERTS = PEER_NKEYS * PEER_NKEYS
PEER_DKEY = 256
PEER_TOPK = 16
PEER_TOKEN_BLOCK = 128
EPS = 1e-6

GLA_QK_W = GLA_HEADS * GLA_DK
GLA_V_W = GLA_HEADS * GLA_DV
MLA_V_W = MLA_HEADS * MLA_DV
D_MIX = GLA_V_W + MLA_V_W
PROJ_SIZES = (GLA_QK_W, GLA_QK_W, GLA_V_W, GLA_V_W, GLA_LOWRANK, MLA_Q_RANK, MLA_KV_RANK, MLA_ROPE)
PROJ_W = GLA_QK_W * 2 + GLA_V_W * 2 + GLA_LOWRANK + MLA_Q_RANK + MLA_KV_RANK + MLA_ROPE

kernel_name = 'hymba_gla_mla_peer_step'


def rmsnorm(x, g):
    xf = x.astype(jnp.float32)
    y = xf * lax.rsqrt(jnp.mean(xf * xf, axis=-1, keepdims=True) + EPS)
    return (y * g.astype(jnp.float32)).astype(x.dtype)


def rope(x, pos):
    half = x.shape[-1] // 2
    inv = ROPE_THETA ** (-jnp.arange(half, dtype=jnp.float32) / half)
    ang = pos.astype(jnp.float32)[:, None] * inv[None, :]
    cos = jnp.cos(ang)[:, None, :]
    sin = jnp.sin(ang)[:, None, :]
    x1 = x[..., :half].astype(jnp.float32)
    x2 = x[..., half:].astype(jnp.float32)
    return jnp.concatenate([x1 * cos - x2 * sin, x2 * cos + x1 * sin], axis=-1).astype(x.dtype)


def split_proj(p):
    outs = []
    off = 0
    for n in PROJ_SIZES:
        outs.append(p[..., off:off + n])
        off += n
    return outs


def gla_chunk(S, q, k, v, g):
    q = q.astype(jnp.float32)
    k = k.astype(jnp.float32)
    v = v.astype(jnp.float32)
    g = g.astype(jnp.float32)
    C = q.shape[2]
    b = jnp.cumsum(g, axis=2)
    causal = jnp.tril(jnp.ones((C, C), dtype=bool))
    diff = b[:, :, :, None, :] - b[:, :, None, :, :]
    decay = jnp.exp(jnp.where(causal[None, None, :, :, None], diff, -jnp.inf))
    A = jnp.einsum('bhid,bhjd,bhijd->bhij', q, k, decay)
    o = jnp.einsum('bhij,bhjv->bhiv', A, v) + jnp.einsum('bhid,bhdv->bhiv', q * jnp.exp(b), S)
    b_last = b[:, :, -1, :]
    S_new = jnp.exp(b_last)[..., None] * S + jnp.einsum('bhjd,bhjv->bhdv', k * jnp.exp(b_last[:, :, None, :] - b), v)
    return S_new, o


def gla_prompt(q, k, v, g):
    B, T, H, _ = q.shape
    n_chunks = T // GLA_CHUNK

    def to_chunks(a):
        return a.reshape(B, n_chunks, GLA_CHUNK, H, a.shape[-1]).transpose(1, 0, 3, 2, 4)

    S0 = jnp.zeros((B, H, GLA_DK, GLA_DV), jnp.float32)
    S, o = lax.scan(lambda S, xs: gla_chunk(S, *xs), S0, (to_chunks(q), to_chunks(k), to_chunks(v), to_chunks(g)))
    o = o.transpose(1, 0, 3, 2, 4).reshape(B, T, H, GLA_DV)
    return o, S


def gla_sample(q, k, v, g, S_past):
    tr = lambda a: a.transpose(0, 2, 1, 3)
    S, o = gla_chunk(S_past.astype(jnp.float32), tr(q), tr(k), tr(v), tr(g))
    return tr(o), S


def mla_prompt(q_nope, q_pe, ckv, kpe, w_uk, w_uv):
    B, T = q_nope.shape[:2]
    k_nope = jnp.einsum('btc,chd->bthd', ckv, w_uk)
    v = jnp.einsum('btc,chd->bthd', ckv, w_uv)
    key_pos = jnp.arange(T)

    def blk(i):
        start = i * Q_BLOCK
        qn = lax.dynamic_slice_in_dim(q_nope, start, Q_BLOCK, axis=1)
        qp = lax.dynamic_slice_in_dim(q_pe, start, Q_BLOCK, axis=1)
        s = (jnp.einsum('bqhd,bkhd->bhqk', qn, k_nope) + jnp.einsum('bqhr,bkr->bhqk', qp, kpe)).astype(jnp.float32) * MLA_SCALE
        mask = (start + jnp.arange(Q_BLOCK))[:, None] >= key_pos[None, :]
        s = jnp.where(mask[None, None], s, -jnp.inf)
        p = jax.nn.softmax(s, axis=-1).astype(v.dtype)
        return jnp.einsum('bhqk,bkhd->bqhd', p, v)

    o = lax.map(blk, jnp.arange(T // Q_BLOCK))
    return o.transpose(1, 0, 2, 3, 4).reshape(B, T, MLA_HEADS, MLA_DV)


def mla_sample(q_nope, q_pe, ckv, kpe, w_uk, w_uv, cache_ckv, cache_kpe, page_table, layer):
    q_lat = jnp.einsum('bqhd,chd->bqhc', q_nope, w_uk)

    def one_seq(args):
        pages, ql, qp, cn, kn = args
        ckv_past = cache_ckv[layer, pages].reshape(-1, MLA_KV_RANK)
        kpe_past = cache_kpe[layer, pages].reshape(-1, MLA_ROPE)
        P = ckv_past.shape[0]
        Sq = ql.shape[0]
        ckv_all = jnp.concatenate([ckv_past, cn.astype(ckv_past.dtype)], axis=0)
        kpe_all = jnp.concatenate([kpe_past, kn.astype(kpe_past.dtype)], axis=0)
        s = (jnp.einsum('qhc,kc->hqk', ql, ckv_all) + jnp.einsum('qhr,kr->hqk', qp, kpe_all)).astype(jnp.float32) * MLA_SCALE
        mask = jnp.arange(P + Sq)[None, :] <= (P + jnp.arange(Sq))[:, None]
        s = jnp.where(mask[None], s, -jnp.inf)
        p = jax.nn.softmax(s, axis=-1).astype(ckv_all.dtype)
        return jnp.einsum('hqk,kc->qhc', p, ckv_all)

    o_lat = lax.map(one_seq, (page_table, q_lat, q_pe, ckv, kpe))
    return jnp.einsum('bqhc,chd->bqhd', o_lat, w_uv)


def peer(x, w_q, keys, u, v):
    T = x.shape[0]
    q = jnp.einsum('td,dp->tp', x, w_q).reshape(T, PEER_HEADS, 2, PEER_DKEY // 2)
    s = jnp.einsum('thpd,hpnd->thpn', q, keys).astype(jnp.float32)
    s1, i1 = lax.top_k(s[:, :, 0], PEER_TOPK)
    s2, i2 = lax.top_k(s[:, :, 1], PEER_TOPK)
    cand = (s1[..., :, None] + s2[..., None, :]).reshape(T, PEER_HEADS, PEER_TOPK * PEER_TOPK)
    cidx = (i1[..., :, None] * PEER_NKEYS + i2[..., None, :]).reshape(T, PEER_HEADS, PEER_TOPK * PEER_TOPK)
    top_s, sel = lax.top_k(cand, PEER_TOPK)
    idx = jnp.take_along_axis(cidx, sel, axis=-1)
    gate = jax.nn.softmax(top_s, axis=-1).astype(x.dtype)
    n_blocks = -(-T // PEER_TOKEN_BLOCK)
    pad = n_blocks * PEER_TOKEN_BLOCK - T
    xp = jnp.pad(x, ((0, pad), (0, 0))).reshape(n_blocks, PEER_TOKEN_BLOCK, D_MODEL)
    ip = jnp.pad(idx, ((0, pad), (0, 0), (0, 0))).reshape(n_blocks, PEER_TOKEN_BLOCK, PEER_HEADS, PEER_TOPK)
    gp = jnp.pad(gate, ((0, pad), (0, 0), (0, 0))).reshape(n_blocks, PEER_TOKEN_BLOCK, PEER_HEADS, PEER_TOPK)

    def blk(args):
        xb, ib, gb = args
        h = jax.nn.gelu(jnp.einsum('td,thkd->thk', xb, u[ib]), approximate=False)
        return jnp.einsum('thk,thkd->td', gb * h, v[ib])

    out = lax.map(blk, (xp, ip, gp))
    return out.reshape(n_blocks * PEER_TOKEN_BLOCK, D_MODEL)[:T]


def layer_forward(x, pos, gla_mix, mla_mix, norm_attn, w_in, gla_w_alpha, gla_b_alpha, gla_norm, mla_q_norm, mla_w_uq, mla_kv_norm, mla_w_uk, mla_w_uv, w_out, norm_ffn, peer_w_q, peer_keys, peer_u, peer_v):
    B, T, _ = x.shape
    h = rmsnorm(x, norm_attn)
    p = jnp.einsum('btd,dp->btp', h, w_in)
    gq, gk, gv, gr, ga, cq, ckv_raw, kpe_raw = split_proj(p)
    gla_q = gq.reshape(B, T, GLA_HEADS, GLA_DK) * (GLA_DK ** -0.5)
    gla_k = gk.reshape(B, T, GLA_HEADS, GLA_DK)
    gla_v = gv.reshape(B, T, GLA_HEADS, GLA_DV)
    gla_g = (jax.nn.log_sigmoid((jnp.einsum('btr,rk->btk', ga, gla_w_alpha) + gla_b_alpha).astype(jnp.float32)) / GLA_TAU).reshape(B, T, GLA_HEADS, GLA_DK)
    q = jnp.einsum('btc,cp->btp', rmsnorm(cq, mla_q_norm), mla_w_uq).reshape(B, T, MLA_HEADS, MLA_NOPE + MLA_ROPE)
    q_nope = q[..., :MLA_NOPE]
    q_pe = rope(q[..., MLA_NOPE:], pos)
    ckv = rmsnorm(ckv_raw, mla_kv_norm)
    kpe = rope(kpe_raw[:, :, None, :], pos)[:, :, 0, :]
    gla_o, gla_state = gla_mix(gla_q, gla_k, gla_v, gla_g)
    mla_o = mla_mix(q_nope, q_pe, ckv, kpe, mla_w_uk, mla_w_uv)
    go = rmsnorm(gla_o.astype(x.dtype), gla_norm).reshape(B, T, GLA_V_W) * jax.nn.silu(gr)
    mix = jnp.concatenate([go, mla_o.reshape(B, T, MLA_V_W).astype(x.dtype)], axis=-1)
    x = x + jnp.einsum('btm,md->btd', mix, w_out)
    y = peer(rmsnorm(x, norm_ffn).reshape(B * T, D_MODEL), peer_w_q, peer_keys, peer_u, peer_v).reshape(B, T, D_MODEL)
    return x + y, ckv, kpe, gla_state


def setup_inputs(seed: int = 0) -> dict:
    key = jax.random.key(seed)
    ks = jax.random.split(key, 24)
    f32 = jnp.float32

    def nrm(k, shape, scale):
        return jax.random.normal(k, shape, f32) * scale

    def gain(k, shape):
        return 1.0 + 0.02 * jax.random.normal(k, shape, f32)

    n_pages = PAST_LEN // PAGE_SIZE
    n_used = DEC_BATCH * n_pages
    n_pool = n_used + n_used // 4
    perm = jax.random.permutation(ks[0], n_pool)
    page_table = perm[:n_used].reshape(DEC_BATCH, n_pages).astype(jnp.int32)
    return {
        'x_prompt': nrm(ks[1], (BATCH, SEQ, D_MODEL), 1.0),
        'x_sample': nrm(ks[2], (DEC_BATCH, DEC_SEQ, D_MODEL), 1.0),
        'cache_ckv': nrm(ks[3], (DEPTH, n_pool, PAGE_SIZE, MLA_KV_RANK), 1.0),
        'cache_kpe': nrm(ks[4], (DEPTH, n_pool, PAGE_SIZE, MLA_ROPE), 1.0),
        'state_gla': nrm(ks[5], (DEPTH, DEC_BATCH, GLA_HEADS, GLA_DK, GLA_DV), 1.0),
        'page_table': page_table,
        'norm_attn': gain(ks[6], (DEPTH, D_MODEL)),
        'w_in': nrm(ks[7], (DEPTH, D_MODEL, PROJ_W), D_MODEL ** -0.5),
        'gla_w_alpha': nrm(ks[8], (DEPTH, GLA_LOWRANK, GLA_QK_W), GLA_LOWRANK ** -0.5),
        'gla_b_alpha': nrm(ks[9], (DEPTH, GLA_QK_W), 0.1),
        'gla_norm': gain(ks[10], (DEPTH, GLA_DV)),
        'mla_q_norm': gain(ks[11], (DEPTH, MLA_Q_RANK)),
        'mla_w_uq': nrm(ks[12], (DEPTH, MLA_Q_RANK, MLA_HEADS * (MLA_NOPE + MLA_ROPE)), MLA_Q_RANK ** -0.5),
        'mla_kv_norm': gain(ks[13], (DEPTH, MLA_KV_RANK)),
        'mla_w_uk': nrm(ks[14], (DEPTH, MLA_KV_RANK, MLA_HEADS, MLA_NOPE), MLA_KV_RANK ** -0.5),
        'mla_w_uv': nrm(ks[15], (DEPTH, MLA_KV_RANK, MLA_HEADS, MLA_DV), MLA_KV_RANK ** -0.5),
        'w_out': nrm(ks[16], (DEPTH, D_MIX, D_MODEL), D_MIX ** -0.5),
        'norm_ffn': gain(ks[17], (DEPTH, D_MODEL)),
        'peer_w_q': nrm(ks[18], (DEPTH, D_MODEL, PEER_HEADS * PEER_DKEY), D_MODEL ** -0.5),
        'peer_keys': nrm(ks[19], (DEPTH, PEER_HEADS, 2, PEER_NKEYS, PEER_DKEY // 2), (PEER_DKEY // 2) ** -0.5),
        'peer_u': nrm(ks[20], (DEPTH, PEER_NEXPERTS, D_MODEL), D_MODEL ** -0.5),
        'peer_v': nrm(ks[21], (DEPTH, PEER_NEXPERTS, D_MODEL), (PEER_HEADS * PEER_TOPK) ** -0.5),
        'norm_final': gain(ks[22], (D_MODEL,)),
    }


def reference(x_prompt, x_sample, cache_ckv, cache_kpe, state_gla, page_table, norm_attn, w_in, gla_w_alpha, gla_b_alpha, gla_norm, mla_q_norm, mla_w_uq, mla_kv_norm, mla_w_uk, mla_w_uv, w_out, norm_ffn, peer_w_q, peer_keys, peer_u, peer_v, norm_final):
    seq = x_prompt.shape[1]
    dec_seq = x_sample.shape[1]
    past_len = page_table.shape[1] * PAGE_SIZE
    pos_prompt = jnp.arange(seq, dtype=jnp.int32)
    pos_sample = past_len + jnp.arange(dec_seq, dtype=jnp.int32)
    xp = x_prompt
    xs = x_sample
    ckv_p_list, kpe_p_list, gla_p_list = [], [], []
    ckv_s_list, kpe_s_list, gla_s_list = [], [], []
    for l in range(DEPTH):
        w = (norm_attn[l], w_in[l], gla_w_alpha[l], gla_b_alpha[l], gla_norm[l], mla_q_norm[l], mla_w_uq[l], mla_kv_norm[l], mla_w_uk[l], mla_w_uv[l], w_out[l], norm_ffn[l], peer_w_q[l], peer_keys[l], peer_u[l], peer_v[l])
        xp, ckv_p, kpe_p, gla_p = layer_forward(xp, pos_prompt, gla_prompt, mla_prompt, *w)
        gla_mix_s = functools.partial(gla_sample, S_past=state_gla[l])
        mla_mix_s = functools.partial(mla_sample, cache_ckv=cache_ckv, cache_kpe=cache_kpe, page_table=page_table, layer=l)
        xs, ckv_s, kpe_s, gla_s = layer_forward(xs, pos_sample, gla_mix_s, mla_mix_s, *w)
        ckv_p_list.append(ckv_p)
        kpe_p_list.append(kpe_p)
        gla_p_list.append(gla_p)
        ckv_s_list.append(ckv_s)
        kpe_s_list.append(kpe_s)
        gla_s_list.append(gla_s)
    y_prompt = rmsnorm(xp, norm_final)
    y_sample = rmsnorm(xs, norm_final)
    return (y_prompt, y_sample, jnp.stack(ckv_p_list), jnp.stack(kpe_p_list), jnp.stack(gla_p_list), jnp.stack(ckv_s_list), jnp.stack(kpe_s_list), jnp.stack(gla_s_list))
```

```python
import functools

import jax
import jax.numpy as jnp
from jax import lax
from jax.experimental import pallas as pl
from jax.experimental.pallas import tpu as pltpu

F32 = jnp.float32
BF16 = jnp.bfloat16

D_MODEL = 2048
PAGE_SIZE = 128
GLA_HEADS = 4
GLA_DK = 128
GLA_DV = 256
GLA_LOWRANK = 16
GLA_TAU = 16.0
MLA_HEADS = 8
MLA_Q_RANK = 512
MLA_KV_RANK = 512
MLA_NOPE = 128
MLA_ROPE = 64
MLA_DV = 128
MLA_SCALE = (MLA_NOPE + MLA_ROPE) ** -0.5
ROPE_THETA = 10000.0
PEER_HEADS = 8
PEER_NKEYS = 128
PEER_DKEY = 256
PEER_TOPK = 16
EPS = 1e-6

GLA_QK_W = GLA_HEADS * GLA_DK
GLA_V_W = GLA_HEADS * GLA_DV
MLA_V_W = MLA_HEADS * MLA_DV
MLA_QHEAD_PAD = 256
LANES = 128

VMEM_LIMIT_BYTES = 56 * 1024 * 1024

PROJ_TOKENS = 320
KV_TOKENS = 512
ATTN_BLOCK = 512
GLA_TOKENS = 256
GLA_SUB = 16
DEC_PAGES = 16
GLA_S_DK_BLOCK = 32
OUT_TOKENS = 320
PEERQ_TOKENS = 128
PEER_TOKENS = 320
PEER_EXPERTS = 512
PEER_G_PAD = 8


def _params(*sem):
    return pltpu.CompilerParams(dimension_semantics=sem, vmem_limit_bytes=VMEM_LIMIT_BYTES)


def _const_spec(shape):
    nd = len(shape)
    return pl.BlockSpec(shape, lambda *_: (0,) * nd, pipeline_mode=pl.Buffered(1))


def _rms(x, g):
    return x * lax.rsqrt(jnp.mean(x * x, axis=-1, keepdims=True) + EPS) * g


def _gelu(x):
    return 0.5 * x * (1.0 + lax.erf(x * (0.5 ** 0.5)))


def _dot(a, b):
    return jnp.dot(a, b, preferred_element_type=F32)


def _dot_nt(a, b):
    return lax.dot_general(a, b, (((1,), (1,)), ((), ())), preferred_element_type=F32)


def _dot_tn(a, b):
    return lax.dot_general(a, b, (((0,), (0,)), ((), ())), preferred_element_type=F32)


_C_GQ, _C_GK, _C_GV, _C_GR = 0, 512, 1024, 2048
_C_CQ, _C_CKV, _C_KPE, _C_KSW, _C_END = 3072, 3584, 4096, 4224, 4352


def _proj_kernel(x_ref, gattn_ref, win_ref, walpha_ref, balpha_ref, gqn_ref, wuq_ref, wuqs_ref, gkv_ref,
                 cq_ref, sq_ref, ck_ref, sk_ref,
                 oq_ref, ok_ref, ov_ref, or_ref, og_ref, oqm_ref, ockv_ref, okpe_ref):
    h = _rms(x_ref[...], gattn_ref[...]).astype(BF16)

    def col(a, b):
        return _dot(h, win_ref[:, a:b])

    oq_ref[...] = col(_C_GQ, _C_GK) * (GLA_DK ** -0.5)
    ok_ref[...] = col(_C_GK, _C_GV)
    ov_ref[...] = col(_C_GV, _C_GR)
    or_ref[...] = col(_C_GR, _C_CQ)
    grp = col(_C_KPE, _C_KSW)
    swp = col(_C_KSW, _C_END)
    xa = _dot(grp.astype(BF16), walpha_ref[...]) + balpha_ref[...]
    og_ref[...] = jax.nn.log_sigmoid(xa) / GLA_TAU
    okpe_ref[...] = grp * ck_ref[...] + swp * sk_ref[...]
    ockv_ref[...] = _rms(col(_C_CKV, _C_KPE), gkv_ref[...])
    cqn = _rms(col(_C_CQ, _C_CKV), gqn_ref[...]).astype(BF16)
    cq_tab = cq_ref[...]
    sq_tab = sq_ref[...]
    for hd in range(MLA_HEADS):
        a, b = hd * MLA_QHEAD_PAD, (hd + 1) * MLA_QHEAD_PAD
        raw = _dot(cqn, wuq_ref[:, a:b])
        sw = _dot(cqn, wuqs_ref[:, a:b])
        oqm_ref[:, a:b] = (raw * cq_tab + sw * sq_tab).astype(BF16)


def _proj(x_all, tabs, gattn, win_p, walpha_p, balpha, gqn, wuq_p, wuqs_p, gkv):
    n = x_all.shape[0]
    tb = PROJ_TOKENS
    cq_tab, sq_tab, ck_tab, sk_tab = tabs

    def rows(w):
        return pl.BlockSpec((tb, w), lambda i: (i, 0))

    out_w = (GLA_QK_W, GLA_QK_W, GLA_V_W, GLA_V_W, GLA_QK_W, MLA_HEADS * MLA_QHEAD_PAD, MLA_KV_RANK, LANES)
    out_dt = (F32, F32, F32, F32, F32, BF16, F32, F32)
    return pl.pallas_call(
        _proj_kernel,
        grid=(n // tb,),
        in_specs=[rows(D_MODEL), _const_spec(gattn.shape), _const_spec(win_p.shape), _const_spec(walpha_p.shape),
                  _const_spec(balpha.shape), _const_spec(gqn.shape), _const_spec(wuq_p.shape),
                  _const_spec(wuqs_p.shape), _const_spec(gkv.shape),
                  rows(MLA_QHEAD_PAD), rows(MLA_QHEAD_PAD), rows(LANES), rows(LANES)],
        out_specs=[rows(w) for w in out_w],
        out_shape=[jax.ShapeDtypeStruct((n, w), dt) for w, dt in zip(out_w, out_dt)],
        compiler_params=_params("parallel"),
    )(x_all, gattn, win_p, walpha_p, balpha, gqn, wuq_p, wuqs_p, gkv, cq_tab, sq_tab, ck_tab, sk_tab)


def _kv_kernel(ckv_ref, kpe_ref, wuk_ref, wuv_ref, k_ref, v_ref):
    c = ckv_ref[...].astype(BF16)
    kpe = kpe_ref[...].astype(BF16)
    kn = _dot(c, wuk_ref[...]).astype(BF16)
    v_ref[...] = _dot(c, wuv_ref[...]).astype(BF16)
    for hd in range(MLA_HEADS):
        a = hd * MLA_QHEAD_PAD
        k_ref[:, a:a + MLA_NOPE] = kn[:, hd * MLA_NOPE:(hd + 1) * MLA_NOPE]
        k_ref[:, a + MLA_NOPE:a + MLA_QHEAD_PAD] = kpe


def _kv_up(ckv, kpe128, wuk2, wuv2, n_prompt):
    tb = KV_TOKENS
    return pl.pallas_call(
        _kv_kernel,
        grid=(n_prompt // tb,),
        in_specs=[pl.BlockSpec((tb, MLA_KV_RANK), lambda i: (i, 0)), pl.BlockSpec((tb, LANES), lambda i: (i, 0)),
                  _const_spec(wuk2.shape), _const_spec(wuv2.shape)],
        out_specs=[pl.BlockSpec((tb, MLA_HEADS * MLA_QHEAD_PAD), lambda i: (i, 0)),
                   pl.BlockSpec((tb, MLA_V_W), lambda i: (i, 0))],
        out_shape=[jax.ShapeDtypeStruct((n_prompt, MLA_HEADS * MLA_QHEAD_PAD), BF16),
                   jax.ShapeDtypeStruct((n_prompt, MLA_V_W), BF16)],
        compiler_params=_params("parallel"),
    )(ckv, kpe128, wuk2, wuv2)


def _mla_prompt_kernel(q_ref, k_ref, v_ref, o_ref, m_scr, l_scr, acc_scr):
    blk = ATTN_BLOCK
    i = pl.program_id(2)
    q = q_ref[...]
    m_scr[...] = jnp.full(m_scr.shape, -jnp.inf, F32)
    l_scr[...] = jnp.zeros(l_scr.shape, F32)
    acc_scr[...] = jnp.zeros(acc_scr.shape, F32)

    def step(kb, masked):
        r0 = pl.multiple_of(kb * blk, blk)
        s = _dot_nt(q, k_ref[pl.ds(r0, blk), :]) * MLA_SCALE
        if masked:
            row = lax.broadcasted_iota(jnp.int32, (blk, blk), 0)
            colm = lax.broadcasted_iota(jnp.int32, (blk, blk), 1)
            s = jnp.where(row >= colm, s, -jnp.inf)
        m_old = m_scr[...]
        m_new = jnp.maximum(m_old, jnp.max(s, axis=-1, keepdims=True))
        alpha = jnp.exp(m_old - m_new)
        p = jnp.exp(s - m_new)
        l_scr[...] = alpha * l_scr[...] + jnp.sum(p, axis=-1, keepdims=True)
        acc_scr[...] = alpha * acc_scr[...] + _dot(p.astype(BF16), v_ref[pl.ds(r0, blk), :])
        m_scr[...] = m_new

    def body(kb, carry):
        step(kb, False)
        return carry

    lax.fori_loop(0, i, body, 0)
    step(i, True)
    o_ref[...] = acc_scr[...] / l_scr[...]


def _mla_prompt(qm, kfull, v, batch, seq):
    blk = ATTN_BLOCK
    nq = seq // blk
    return pl.pallas_call(
        _mla_prompt_kernel,
        grid=(batch, MLA_HEADS, nq),
        in_specs=[pl.BlockSpec((blk, MLA_QHEAD_PAD), lambda b, h, i: (b * nq + i, h)),
                  pl.BlockSpec((seq, MLA_QHEAD_PAD), lambda b, h, i: (b, h)),
                  pl.BlockSpec((seq, MLA_DV), lambda b, h, i: (b, h))],
        out_specs=pl.BlockSpec((blk, MLA_DV), lambda b, h, i: (b * nq + i, h)),
        out_shape=jax.ShapeDtypeStruct((batch * seq, MLA_V_W), F32),
        scratch_shapes=[pltpu.VMEM((blk, 1), F32), pltpu.VMEM((blk, 1), F32), pltpu.VMEM((blk, MLA_DV), F32)],
        compiler_params=_params("parallel", "parallel", "arbitrary"),
    )(qm, kfull, v)


def _qlat_kernel(q_ref, wuk_ref, o_ref):
    q = q_ref[...]
    qlat = _dot_nt(q[:, :MLA_NOPE], wuk_ref[0])
    o_ref[0] = jnp.concatenate([qlat, q[:, MLA_NOPE:MLA_NOPE + MLA_ROPE].astype(F32)], axis=-1).astype(BF16)


def _qlat(qm, wuk_h, n_prompt, n_sample):
    rb = n_prompt // n_sample
    width = MLA_KV_RANK + MLA_ROPE
    return pl.pallas_call(
        _qlat_kernel,
        grid=(MLA_HEADS,),
        in_specs=[pl.BlockSpec((n_sample, MLA_QHEAD_PAD), lambda h: (rb, h)),
                  pl.BlockSpec((1, MLA_KV_RANK, MLA_NOPE), lambda h: (h, 0, 0))],
        out_specs=pl.BlockSpec((1, n_sample, width), lambda h: (h, 0, 0)),
        out_shape=jax.ShapeDtypeStruct((MLA_HEADS, n_sample, width), BF16),
        compiler_params=_params("parallel"),
    )(qm, wuk_h)


def _decode_kernel(pt_ref, q_ref, cnew_ref, knew_ref, *refs):
    np_ = DEC_PAGES
    ckv_refs = refs[:np_]
    kpe_refs = refs[np_:2 * np_]
    o_ref = refs[2 * np_]
    kbuf, m_scr, l_scr, acc_scr = refs[2 * np_ + 1:]
    c = pl.program_id(1)
    rank = MLA_KV_RANK

    @pl.when(c == 0)
    def _():
        m_scr[...] = jnp.full(m_scr.shape, -jnp.inf, F32)
        l_scr[...] = jnp.zeros(l_scr.shape, F32)
        acc_scr[...] = jnp.zeros(acc_scr.shape, F32)

    for p in range(np_):
        kbuf[p * PAGE_SIZE:(p + 1) * PAGE_SIZE, :rank] = ckv_refs[p][0, 0].astype(BF16)
        kbuf[p * PAGE_SIZE:(p + 1) * PAGE_SIZE, rank:] = kpe_refs[p][0, 0].astype(BF16)
    q = q_ref[0]
    s = _dot_nt(q, kbuf[...]) * MLA_SCALE
    m_old = m_scr[...]
    m_new = jnp.maximum(m_old, jnp.max(s, axis=-1, keepdims=True))
    alpha = jnp.exp(m_old - m_new)
    p = jnp.exp(s - m_new)
    l_scr[...] = alpha * l_scr[...] + jnp.sum(p, axis=-1, keepdims=True)
    acc_scr[...] = alpha * acc_scr[...] + _dot(p.astype(BF16), kbuf[:, :rank])
    m_scr[...] = m_new

    @pl.when(c == pl.num_programs(1) - 1)
    def _():
        cn = cnew_ref[0].astype(BF16).astype(F32)
        kn = knew_ref[0][:, :MLA_ROPE].astype(BF16).astype(F32)
        qf = q.astype(F32)
        s_new = (jnp.sum(qf[:, :rank] * cn, axis=-1, keepdims=True)
                 + jnp.sum(qf[:, rank:] * kn, axis=-1, keepdims=True)) * MLA_SCALE
        m_old2 = m_scr[...]
        m_fin = jnp.maximum(m_old2, s_new)
        a2 = jnp.exp(m_old2 - m_fin)
        p_new = jnp.exp(s_new - m_fin)
        l_fin = a2 * l_scr[...] + p_new
        acc = a2 * acc_scr[...] + p_new.astype(BF16).astype(F32) * cn
        o_ref[0] = acc / l_fin


def _decode(page_table, qfull, ckv_new, kpe_new, cache_ckv, cache_kpe):
    n_seq, n_pages = page_table.shape
    np_ = DEC_PAGES
    width = MLA_KV_RANK + MLA_ROPE

    def page_spec(last, p):
        return pl.BlockSpec((1, 1, PAGE_SIZE, last), lambda s, c, pt: (0, pt[s, c * np_ + p], 0, 0))

    grid_spec = pltpu.PrefetchScalarGridSpec(
        num_scalar_prefetch=1,
        grid=(n_seq, n_pages // np_),
        in_specs=[pl.BlockSpec((1, MLA_HEADS, width), lambda s, c, pt: (s, 0, 0)),
                  pl.BlockSpec((1, 1, MLA_KV_RANK), lambda s, c, pt: (s, 0, 0)),
                  pl.BlockSpec((1, 1, LANES), lambda s, c, pt: (s, 0, 0))]
                 + [page_spec(MLA_KV_RANK, p) for p in range(np_)]
                 + [page_spec(MLA_ROPE, p) for p in range(np_)],
        out_specs=pl.BlockSpec((1, MLA_HEADS, MLA_KV_RANK), lambda s, c, pt: (s, 0, 0)),
        scratch_shapes=[pltpu.VMEM((np_ * PAGE_SIZE, width), BF16), pltpu.VMEM((MLA_HEADS, 1), F32),
                        pltpu.VMEM((MLA_HEADS, 1), F32), pltpu.VMEM((MLA_HEADS, MLA_KV_RANK), F32)],
    )
    return pl.pallas_call(
        _decode_kernel,
        grid_spec=grid_spec,
        out_shape=jax.ShapeDtypeStruct((n_seq, MLA_HEADS, MLA_KV_RANK), F32),
        compiler_params=_params("parallel", "arbitrary"),
    )(page_table, qfull, ckv_new, kpe_new, *([cache_ckv] * np_), *([cache_kpe] * np_))


def _oup_kernel(o_ref, wuv_ref, out_ref):
    out_ref[...] = _dot(o_ref[0].astype(BF16), wuv_ref[0])


def _o_up(olat_h, wuv_h):
    n_sample = olat_h.shape[1]
    return pl.pallas_call(
        _oup_kernel,
        grid=(MLA_HEADS,),
        in_specs=[pl.BlockSpec((1, n_sample, MLA_KV_RANK), lambda h: (h, 0, 0)),
                  pl.BlockSpec((1, MLA_KV_RANK, MLA_DV), lambda h: (h, 0, 0))],
        out_specs=pl.BlockSpec((n_sample, MLA_DV), lambda h: (0, h)),
        out_shape=jax.ShapeDtypeStruct((n_sample, MLA_V_W), F32),
        compiler_params=_params("parallel"),
    )(olat_h, wuv_h)


def _gla_prompt_kernel(q_ref, k_ref, v_ref, g_ref, o_ref, sout_ref, st_scr):
    sub = GLA_SUB
    c = pl.program_id(1)

    @pl.when(c == 0)
    def _():
        st_scr[...] = jnp.zeros(st_scr.shape, F32)

    row = lax.broadcasted_iota(jnp.int32, (sub, sub), 0)
    colm = lax.broadcasted_iota(jnp.int32, (sub, sub), 1)
    tri = (row >= colm).astype(F32)
    rowi = lax.broadcasted_iota(jnp.int32, (sub, 1), 0)

    def chunk(ci, carry):
        r0 = pl.multiple_of(ci * sub, sub)
        b_all = jnp.dot(tri, g_ref[pl.ds(r0, sub), :], precision=lax.Precision.HIGHEST,
                        preferred_element_type=F32)
        for hd in range(GLA_HEADS):
            ka, kb = hd * GLA_DK, (hd + 1) * GLA_DK
            va, vb = hd * GLA_DV, (hd + 1) * GLA_DV
            q = q_ref[pl.ds(r0, sub), ka:kb]
            k = k_ref[pl.ds(r0, sub), ka:kb]
            v = v_ref[pl.ds(r0, sub), va:vb]
            b = b_all[:, ka:kb]
            st = st_scr[hd]
            o = _dot_nt((q * jnp.exp(b)).astype(BF16), st.astype(BF16))
            for j in range(sub):
                w = q * k[j:j + 1, :] * jnp.exp(jnp.minimum(b - b[j:j + 1, :], 0.0))
                a = jnp.sum(w, axis=-1, keepdims=True)
                o = o + jnp.where(rowi >= j, a, 0.0) * v[j:j + 1, :]
            o_ref[pl.ds(r0, sub), va:vb] = o
            bl = b[sub - 1:sub, :]
            kt = (k * jnp.exp(bl - b)).astype(BF16)
            st_scr[hd] = st * jnp.exp(bl) + _dot_tn(v.astype(BF16), kt)
        return carry

    lax.fori_loop(0, GLA_TOKENS // sub, chunk, 0)

    @pl.when(c == pl.num_programs(1) - 1)
    def _():
        for hd in range(GLA_HEADS):
            sout_ref[0, hd] = st_scr[hd].T


def _gla_prompt(gq, gk, gv, gg, batch, seq):
    tc = GLA_TOKENS
    nc = seq // tc

    def rows(w):
        return pl.BlockSpec((tc, w), lambda b, c: (b * nc + c, 0))

    return pl.pallas_call(
        _gla_prompt_kernel,
        grid=(batch, nc),
        in_specs=[rows(GLA_QK_W), rows(GLA_QK_W), rows(GLA_V_W), rows(GLA_QK_W)],
        out_specs=[rows(GLA_V_W), pl.BlockSpec((1, GLA_HEADS, GLA_DK, GLA_DV), lambda b, c: (b, 0, 0, 0))],
        out_shape=[jax.ShapeDtypeStruct((batch * seq, GLA_V_W), F32),
                   jax.ShapeDtypeStruct((batch, GLA_HEADS, GLA_DK, GLA_DV), F32)],
        scratch_shapes=[pltpu.VMEM((GLA_HEADS, GLA_DV, GLA_DK), F32)],
        compiler_params=_params("parallel", "arbitrary"),
    )(gq, gk, gv, gg)


def _gla_sample_kernel(qt_ref, kt_ref, gt_ref, v_ref, s_ref, snew_ref, o_ref):
    c = pl.program_id(1)
    n_seq = s_ref.shape[0]

    @pl.when(c == 0)
    def _():
        o_ref[...] = jnp.zeros(o_ref.shape, F32)

    qt = qt_ref[...]
    kt = kt_ref[...]
    at = jnp.exp(gt_ref[...])
    for s in range(n_seq):
        sn = at[:, s:s + 1] * s_ref[s, 0] + kt[:, s:s + 1] * v_ref[s:s + 1, :]
        snew_ref[s, 0] = sn
        o_ref[s:s + 1, :] += jnp.sum(qt[:, s:s + 1] * sn, axis=0, keepdims=True)


def _gla_sample(qt, kt, gt, v, state):
    n_seq = state.shape[0]
    rb = GLA_S_DK_BLOCK
    nb = GLA_DK // rb
    vec = pl.BlockSpec((rb, n_seq), lambda h, c: (h * nb + c, 0))
    st = pl.BlockSpec((n_seq, 1, rb, GLA_DV), lambda h, c: (0, h, c, 0))
    return pl.pallas_call(
        _gla_sample_kernel,
        grid=(GLA_HEADS, nb),
        in_specs=[vec, vec, vec, pl.BlockSpec((n_seq, GLA_DV), lambda h, c: (0, h)), st],
        out_specs=[st, pl.BlockSpec((n_seq, GLA_DV), lambda h, c: (0, h))],
        out_shape=[jax.ShapeDtypeStruct(state.shape, F32), jax.ShapeDtypeStruct((n_seq, GLA_V_W), F32)],
        compiler_params=_params("parallel", "arbitrary"),
    )(qt, kt, gt, v, state)


def _out_kernel(x_ref, go_ref, gr_ref, mo_ref, gn_ref, wout_ref, o_ref):
    gn = gn_ref[...]
    gr = gr_ref[...]
    acc = x_ref[...] + _dot(mo_ref[...].astype(BF16), wout_ref[GLA_V_W:, :])
    parts = []
    for hd in range(GLA_HEADS):
        a, b = hd * GLA_DV, (hd + 1) * GLA_DV
        parts.append(_rms(go_ref[:, a:b], gn) * jax.nn.silu(gr[:, a:b]))
    go = jnp.concatenate(parts, axis=-1).astype(BF16)
    o_ref[...] = acc + _dot(go, wout_ref[:GLA_V_W, :])


def _out_proj(x_all, gla_o, gr, mla_o, gn, wout):
    n = x_all.shape[0]
    tb = OUT_TOKENS

    def rows(w):
        return pl.BlockSpec((tb, w), lambda i: (i, 0))

    return pl.pallas_call(
        _out_kernel,
        grid=(n // tb,),
        in_specs=[rows(D_MODEL), rows(GLA_V_W), rows(GLA_V_W), rows(MLA_V_W), _const_spec(gn.shape),
                  _const_spec(wout.shape)],
        out_specs=rows(D_MODEL),
        out_shape=jax.ShapeDtypeStruct((n, D_MODEL), F32),
        compiler_params=_params("parallel"),
    )(x_all, gla_o, gr, mla_o, gn, wout)


def _topk_rows(x, k):
    n_rows = x.shape[0]
    riota = lax.broadcasted_iota(jnp.int32, x.shape, 0).astype(F32)
    vals, poss = [], []
    for _ in range(k):
        m = jnp.max(x, axis=0, keepdims=True)
        p = jnp.min(jnp.where(x == m, riota, float(n_rows)), axis=0, keepdims=True)
        x = jnp.where(riota == p, -jnp.inf, x)
        vals.append(m)
        poss.append(p)
    return jnp.concatenate(vals, axis=0), jnp.concatenate(poss, axis=0)


def _peerq_kernel(x_ref, gffn_ref, wq_ref, keys_ref, xn_ref, a_ref, b_ref, g_ref,
                  q_scr, v_scr, i_scr, sa_scr, sb_scr, sg_scr):
    topk = PEER_TOPK
    half = PEER_DKEY // 2
    xn = _rms(x_ref[...], gffn_ref[...]).astype(BF16)
    xn_ref[...] = xn
    q = _dot(xn, wq_ref[...])
    for hp in range(2 * PEER_HEADS):
        q_scr[hp] = q[:, hp * half:(hp + 1) * half].astype(BF16)

    def side(hp, carry):
        s = _dot_nt(keys_ref[hp], q_scr[hp])
        vals, poss = _topk_rows(s, topk)
        v_scr[hp] = vals
        i_scr[hp] = poss
        return carry

    lax.fori_loop(0, 2 * PEER_HEADS, side, 0)

    r16 = lax.broadcasted_iota(jnp.int32, (topk, LANES), 0).astype(F32)

    def head(hd, carry):
        s1, s2 = v_scr[2 * hd], v_scr[2 * hd + 1]
        i1, i2 = i_scr[2 * hd], i_scr[2 * hd + 1]
        cand = jnp.concatenate([s1[r:r + 1, :] + s2 for r in range(topk)], axis=0)
        top_s, pos = _topk_rows(cand, topk)
        e = jnp.exp(top_s - top_s[0:1, :])
        gate = e / jnp.sum(e, axis=0, keepdims=True)
        r = jnp.floor(pos * (1.0 / topk))
        cc = pos - r * topk
        a_rows, b_rows = [], []
        for kk in range(topk):
            a_rows.append(jnp.sum(jnp.where(r16 == r[kk:kk + 1, :], i1, 0.0), axis=0, keepdims=True))
            b_rows.append(jnp.sum(jnp.where(r16 == cc[kk:kk + 1, :], i2, 0.0), axis=0, keepdims=True))
        base = pl.multiple_of(hd * topk, topk)
        sa_scr[pl.ds(base, topk), :] = jnp.concatenate(a_rows, axis=0)
        sb_scr[pl.ds(base, topk), :] = jnp.concatenate(b_rows, axis=0)
        sg_scr[pl.ds(base, topk), :] = gate
        return carry

    lax.fori_loop(0, PEER_HEADS, head, 0)
    a_ref[...] = sa_scr[...].T
    b_ref[...] = sb_scr[...].T
    g_ref[...] = sg_scr[...].T


def _peer_query(x2, gffn, wq, keys2):
    n = x2.shape[0]
    tb = PEERQ_TOKENS
    nslot = PEER_HEADS * PEER_TOPK
    half = PEER_DKEY // 2

    def rows(w):
        return pl.BlockSpec((tb, w), lambda i: (i, 0))

    return pl.pallas_call(
        _peerq_kernel,
        grid=(n // tb,),
        in_specs=[rows(D_MODEL), _const_spec(gffn.shape), _const_spec(wq.shape), _const_spec(keys2.shape)],
        out_specs=[rows(D_MODEL), rows(nslot), rows(nslot), rows(nslot)],
        out_shape=[jax.ShapeDtypeStruct((n, D_MODEL), BF16)] + [jax.ShapeDtypeStruct((n, nslot), F32)] * 3,
        scratch_shapes=[pltpu.VMEM((2 * PEER_HEADS, tb, half), BF16),
                        pltpu.VMEM((2 * PEER_HEADS, PEER_TOPK, tb), F32),
                        pltpu.VMEM((2 * PEER_HEADS, PEER_TOPK, tb), F32),
                        pltpu.VMEM((nslot, tb), F32), pltpu.VMEM((nslot, tb), F32), pltpu.VMEM((nslot, tb), F32)],
        compiler_params=_params("parallel"),
    )(x2, gffn, wq, keys2)


def _peer_dense_kernel(xn_ref, a_ref, b_ref, g_ref, ut_ref, v_ref, x2_ref, gfin_ref, out_ref, gp_scr):
    tb = PEER_TOKENS
    te = PEER_EXPERTS
    stride = tb + PEER_G_PAD
    j = pl.program_id(1)

    @pl.when(j == 0)
    def _():
        out_ref[...] = jnp.zeros(out_ref.shape, F32)
        sub_iota = lax.broadcasted_iota(jnp.int32, (PEER_NKEYS, LANES), 0).astype(F32)

        def tok(t, carry):
            arow = a_ref[pl.ds(t, 1), :]
            brow = b_ref[pl.ds(t, 1), :]
            grow = g_ref[pl.ds(t, 1), :]
            at = jnp.where(sub_iota == arow, 1.0, 0.0).astype(BF16)
            bt = jnp.where(sub_iota == brow, grow, 0.0).astype(BF16)
            gp_scr[pl.ds(t, PEER_NKEYS, stride=stride), :] = _dot_nt(at, bt)
            return carry

        lax.fori_loop(0, tb, tok, 0)

    act = _gelu(_dot(xn_ref[...], ut_ref[...]))
    n_a = te // LANES
    parts = []
    for jj in range(n_a):
        r0 = pl.multiple_of((j * n_a + jj) * stride, 8)
        parts.append(gp_scr[pl.ds(r0, tb), :] * act[:, jj * LANES:(jj + 1) * LANES])
    p = jnp.concatenate(parts, axis=-1).astype(BF16)
    out_ref[...] += _dot(p, v_ref[...])

    @pl.when(j == pl.num_programs(1) - 1)
    def _():
        out_ref[...] = _rms(x2_ref[...] + out_ref[...], gfin_ref[...])


def _peer_dense(xn, sa, sb, sg, ut, vv, x2, gfin):
    n = xn.shape[0]
    tb = PEER_TOKENS
    te = PEER_EXPERTS
    n_exp = vv.shape[0]
    nslot = PEER_HEADS * PEER_TOPK

    def rows(w):
        return pl.BlockSpec((tb, w), lambda i, j: (i, 0))

    return pl.pallas_call(
        _peer_dense_kernel,
        grid=(n // tb, n_exp // te),
        in_specs=[rows(D_MODEL), rows(nslot), rows(nslot), rows(nslot),
                  pl.BlockSpec((D_MODEL, te), lambda i, j: (0, j)),
                  pl.BlockSpec((te, D_MODEL), lambda i, j: (j, 0)),
                  rows(D_MODEL), _const_spec(gfin.shape)],
        out_specs=rows(D_MODEL),
        out_shape=jax.ShapeDtypeStruct((n, D_MODEL), F32),
        scratch_shapes=[pltpu.VMEM((PEER_NKEYS * (tb + PEER_G_PAD), LANES), F32)],
        compiler_params=_params("parallel", "arbitrary"),
    )(xn, sa, sb, sg, ut, vv, x2, gfin)


def _rope_tables(pos):
    half = MLA_ROPE // 2
    inv = ROPE_THETA ** (-jnp.arange(half, dtype=F32) / half)
    ang = pos.astype(F32)[:, None] * inv[None, :]
    cos, sin = jnp.cos(ang), jnp.sin(ang)
    n = pos.shape[0]
    cc = jnp.concatenate([cos, cos], axis=-1)
    ss = jnp.concatenate([-sin, sin], axis=-1)
    z64 = jnp.zeros((n, LANES - MLA_ROPE), F32)
    cq_tab = jnp.concatenate([jnp.ones((n, MLA_NOPE), F32), cc, z64], axis=-1)
    sq_tab = jnp.concatenate([jnp.zeros((n, MLA_NOPE), F32), ss, z64], axis=-1)
    ck_tab = jnp.concatenate([cc, z64], axis=-1)
    sk_tab = jnp.concatenate([ss, z64], axis=-1)
    return cq_tab, sq_tab, ck_tab, sk_tab


def _prep_win(w_in):
    d = w_in.shape[0]
    o = 0
    pieces = {}
    for name, width in (("gq", GLA_QK_W), ("gk", GLA_QK_W), ("gv", GLA_V_W), ("gr", GLA_V_W), ("ga", GLA_LOWRANK),
                        ("cq", MLA_Q_RANK), ("ckv", MLA_KV_RANK), ("kpe", MLA_ROPE)):
        pieces[name] = w_in[:, o:o + width]
        o += width
    half = MLA_ROPE // 2
    kpe = pieces["kpe"]
    ksw = jnp.concatenate([kpe[:, half:], kpe[:, :half]], axis=-1)
    z = lambda w: jnp.zeros((d, w), w_in.dtype)
    cols = [pieces["gq"], pieces["gk"], pieces["gv"], pieces["gr"], pieces["cq"], pieces["ckv"],
            kpe, pieces["ga"], z(LANES - MLA_ROPE - GLA_LOWRANK), ksw, z(LANES - MLA_ROPE)]
    return jnp.concatenate(cols, axis=-1).astype(BF16)


def _prep_wuq(w_uq):
    r = w_uq.shape[0]
    w = w_uq.reshape(r, MLA_HEADS, MLA_NOPE + MLA_ROPE)
    half = MLA_ROPE // 2
    nope, pe = w[..., :MLA_NOPE], w[..., MLA_NOPE:]
    pad = jnp.zeros((r, MLA_HEADS, MLA_QHEAD_PAD - MLA_NOPE - MLA_ROPE), w_uq.dtype)
    plain = jnp.concatenate([nope, pe, pad], axis=-1)
    swapped = jnp.concatenate([jnp.zeros_like(nope), pe[..., half:], pe[..., :half], pad], axis=-1)
    width = MLA_HEADS * MLA_QHEAD_PAD
    return plain.reshape(r, width).astype(BF16), swapped.reshape(r, width).astype(BF16)


def kernel(x_prompt, x_sample, cache_ckv, cache_kpe, state_gla, page_table, norm_attn, w_in, gla_w_alpha, gla_b_alpha, gla_norm, mla_q_norm, mla_w_uq, mla_kv_norm, mla_w_uk, mla_w_uv, w_out, norm_ffn, peer_w_q, peer_keys, peer_u, peer_v, norm_final):
    batch, seq, d = x_prompt.shape
    n_seq, dec_seq, _ = x_sample.shape
    depth = w_in.shape[0]
    assert depth == 1 and dec_seq == 1 and d == D_MODEL
    n_prompt = batch * seq
    n = n_prompt + n_seq
    past_len = page_table.shape[1] * PAGE_SIZE

    x_all = jnp.concatenate([x_prompt.reshape(n_prompt, d), x_sample.reshape(n_seq, d)], axis=0)
    pos = jnp.concatenate([jnp.tile(jnp.arange(seq, dtype=jnp.int32), batch),
                           jnp.full((n_seq,), past_len, jnp.int32)])
    tabs = _rope_tables(pos)

    row = lambda g: g.reshape(1, -1).astype(F32)
    win_p = _prep_win(w_in[0])
    walpha_p = jnp.zeros((LANES, GLA_QK_W), F32).at[MLA_ROPE:MLA_ROPE + GLA_LOWRANK].set(gla_w_alpha[0]).astype(BF16)
    wuq_p, wuqs_p = _prep_wuq(mla_w_uq[0])

    gq, gk, gv, gr, gg, qm, ckv, kpe128 = _proj(
        x_all, tabs, row(norm_attn[0]), win_p, walpha_p, row(gla_b_alpha[0]), row(mla_q_norm[0]),
        wuq_p, wuqs_p, row(mla_kv_norm[0]))

    wuk2 = mla_w_uk[0].reshape(MLA_KV_RANK, MLA_HEADS * MLA_NOPE).astype(BF16)
    wuv2 = mla_w_uv[0].reshape(MLA_KV_RANK, MLA_V_W).astype(BF16)
    kfull, vfull = _kv_up(ckv, kpe128, wuk2, wuv2, n_prompt)
    mla_o_p = _mla_prompt(qm, kfull, vfull, batch, seq)

    wuk_h = jnp.transpose(mla_w_uk[0], (1, 0, 2)).astype(BF16)
    wuv_h = jnp.transpose(mla_w_uv[0], (1, 0, 2)).astype(BF16)
    qfull = jnp.transpose(_qlat(qm, wuk_h, n_prompt, n_seq), (1, 0, 2))
    olat = _decode(page_table, qfull, ckv[n_prompt:].reshape(n_seq, 1, MLA_KV_RANK),
                   kpe128[n_prompt:].reshape(n_seq, 1, LANES), cache_ckv, cache_kpe)
    mla_o_s = _o_up(jnp.transpose(olat, (1, 0, 2)), wuv_h)

    gla_o_p, gla_state_p = _gla_prompt(gq, gk, gv, gg, batch, seq)
    gla_state_s, gla_o_s = _gla_sample(gq[n_prompt:].T, gk[n_prompt:].T, gg[n_prompt:].T, gv[n_prompt:],
                                       state_gla[0])

    gla_o = jnp.concatenate([gla_o_p, gla_o_s], axis=0)
    mla_o = jnp.concatenate([mla_o_p, mla_o_s], axis=0)
    x2 = _out_proj(x_all, gla_o, gr, mla_o, row(gla_norm[0]), w_out[0].astype(BF16))

    keys2 = peer_keys[0].reshape(2 * PEER_HEADS, PEER_NKEYS, PEER_DKEY // 2).astype(BF16)
    xn, sa, sb, sg = _peer_query(x2, row(norm_ffn[0]), peer_w_q[0].astype(BF16), keys2)
    y = _peer_dense(xn, sa, sb, sg, peer_u[0].T.astype(BF16), peer_v[0].astype(BF16), x2, row(norm_final))

    y_prompt = y[:n_prompt].reshape(batch, seq, d)
    y_sample = y[n_prompt:].reshape(n_seq, dec_seq, d)
    ckv_p = ckv[:n_prompt].reshape(1, batch, seq, MLA_KV_RANK)
    kpe_p = kpe128[:n_prompt, :MLA_ROPE].reshape(1, batch, seq, MLA_ROPE)
    ckv_s = ckv[n_prompt:].reshape(1, n_seq, dec_seq, MLA_KV_RANK)
    kpe_s = kpe128[n_prompt:, :MLA_ROPE].reshape(1, n_seq, dec_seq, MLA_ROPE)
    return (y_prompt, y_sample, ckv_p, kpe_p, gla_state_p[None], ckv_s, kpe_s, gla_state_s[None])
```

```python
import functools

import jax
import jax.numpy as jnp
from jax import lax
from jax.experimental import pallas as pl
from jax.experimental.pallas import tpu as pltpu

F32 = jnp.float32
BF16 = jnp.bfloat16

D_MODEL = 2048
PAGE_SIZE = 128
GLA_HEADS = 4
GLA_DK = 128
GLA_DV = 256
GLA_LOWRANK = 16
GLA_TAU = 16.0
MLA_HEADS = 8
MLA_Q_RANK = 512
MLA_KV_RANK = 512
MLA_NOPE = 128
MLA_ROPE = 64
MLA_DV = 128
MLA_SCALE = (MLA_NOPE + MLA_ROPE) ** -0.5
ROPE_THETA = 10000.0
PEER_HEADS = 8
PEER_NKEYS = 128
PEER_DKEY = 256
PEER_TOPK = 16
EPS = 1e-6

GLA_QK_W = GLA_HEADS * GLA_DK
GLA_V_W = GLA_HEADS * GLA_DV
MLA_V_W = MLA_HEADS * MLA_DV
MLA_QHEAD_PAD = 256
LANES = 128

VMEM_LIMIT_BYTES = 56 * 1024 * 1024

PROJ_TOKENS = 320
KV_TOKENS = 512
ATTN_BLOCK = 512
ATTN_HEADS = 4
GLA_TOKENS = 256
GLA_SUB = 16
DEC_PAGES = 32
GLA_S_DK_BLOCK = 32
OUT_TOKENS = 320
PEERQ_TOKENS = 128
PEER_TOKENS = 320
PEER_EXPERTS = 512
PEER_G_PAD = 8
PEER_G_UNROLL = 8


def _params(*sem):
    return pltpu.CompilerParams(dimension_semantics=sem, vmem_limit_bytes=VMEM_LIMIT_BYTES)


def _const_spec(shape):
    nd = len(shape)
    return pl.BlockSpec(shape, lambda *_: (0,) * nd, pipeline_mode=pl.Buffered(1))


def _rms(x, g):
    return x * lax.rsqrt(jnp.mean(x * x, axis=-1, keepdims=True) + EPS) * g


def _gelu(x):
    return 0.5 * x * (1.0 + lax.erf(x * (0.5 ** 0.5)))


def _dot(a, b):
    return jnp.dot(a, b, preferred_element_type=F32)


def _dot_nt(a, b):
    return lax.dot_general(a, b, (((1,), (1,)), ((), ())), preferred_element_type=F32)


def _dot_tn(a, b):
    return lax.dot_general(a, b, (((0,), (0,)), ((), ())), preferred_element_type=F32)


_C_GQ, _C_GK, _C_GV, _C_GR = 0, 512, 1024, 2048
_C_CQ, _C_CKV, _C_KPE, _C_KSW, _C_END = 3072, 3584, 4096, 4224, 4352


def _proj_kernel(x_ref, gattn_ref, win_ref, walpha_ref, balpha_ref, gqn_ref, wuq_ref, wuqs_ref, gkv_ref,
                 cq_ref, sq_ref, ck_ref, sk_ref,
                 oq_ref, ok_ref, ov_ref, or_ref, og_ref, oqm_ref, ockv_ref, okpe_ref):
    h = _rms(x_ref[...], gattn_ref[...]).astype(BF16)

    def col(a, b):
        return _dot(h, win_ref[:, a:b])

    oq_ref[...] = col(_C_GQ, _C_GK) * (GLA_DK ** -0.5)
    ok_ref[...] = col(_C_GK, _C_GV)
    ov_ref[...] = col(_C_GV, _C_GR)
    or_ref[...] = col(_C_GR, _C_CQ)
    grp = col(_C_KPE, _C_KSW)
    swp = col(_C_KSW, _C_END)
    xa = _dot(grp.astype(BF16), walpha_ref[...]) + balpha_ref[...]
    og_ref[...] = jax.nn.log_sigmoid(xa) / GLA_TAU
    okpe_ref[...] = grp * ck_ref[...] + swp * sk_ref[...]
    ockv_ref[...] = _rms(col(_C_CKV, _C_KPE), gkv_ref[...])
    cqn = _rms(col(_C_CQ, _C_CKV), gqn_ref[...]).astype(BF16)
    cq_tab = cq_ref[...]
    sq_tab = sq_ref[...]
    for hd in range(MLA_HEADS):
        a, b = hd * MLA_QHEAD_PAD, (hd + 1) * MLA_QHEAD_PAD
        raw = _dot(cqn, wuq_ref[:, a:b])
        sw = _dot(cqn, wuqs_ref[:, a:b])
        oqm_ref[:, a:b] = (raw * cq_tab + sw * sq_tab).astype(BF16)


def _proj(x_all, tabs, gattn, win_p, walpha_p, balpha, gqn, wuq_p, wuqs_p, gkv):
    n = x_all.shape[0]
    tb = PROJ_TOKENS
    cq_tab, sq_tab, ck_tab, sk_tab = tabs

    def rows(w):
        return pl.BlockSpec((tb, w), lambda i: (i, 0))

    out_w = (GLA_QK_W, GLA_QK_W, GLA_V_W, GLA_V_W, GLA_QK_W, MLA_HEADS * MLA_QHEAD_PAD, MLA_KV_RANK, LANES)
    out_dt = (F32, F32, F32, F32, F32, BF16, F32, F32)
    return pl.pallas_call(
        _proj_kernel,
        grid=(n // tb,),
        in_specs=[rows(D_MODEL), _const_spec(gattn.shape), _const_spec(win_p.shape), _const_spec(walpha_p.shape),
                  _const_spec(balpha.shape), _const_spec(gqn.shape), _const_spec(wuq_p.shape),
                  _const_spec(wuqs_p.shape), _const_spec(gkv.shape),
                  rows(MLA_QHEAD_PAD), rows(MLA_QHEAD_PAD), rows(LANES), rows(LANES)],
        out_specs=[rows(w) for w in out_w],
        out_shape=[jax.ShapeDtypeStruct((n, w), dt) for w, dt in zip(out_w, out_dt)],
        compiler_params=_params("parallel"),
    )(x_all, gattn, win_p, walpha_p, balpha, gqn, wuq_p, wuqs_p, gkv, cq_tab, sq_tab, ck_tab, sk_tab)


def _kv_kernel(ckv_ref, kpe_ref, wuk_ref, wuv_ref, k_ref, v_ref):
    c = ckv_ref[...].astype(BF16)
    kpe = kpe_ref[...].astype(BF16)
    kn = _dot(c, wuk_ref[...]).astype(BF16)
    v_ref[...] = _dot(c, wuv_ref[...]).astype(BF16)
    for hd in range(MLA_HEADS):
        a = hd * MLA_QHEAD_PAD
        k_ref[:, a:a + MLA_NOPE] = kn[:, hd * MLA_NOPE:(hd + 1) * MLA_NOPE]
        k_ref[:, a + MLA_NOPE:a + MLA_QHEAD_PAD] = kpe


def _kv_up(ckv, kpe128, wuk2, wuv2, n_prompt):
    tb = KV_TOKENS
    return pl.pallas_call(
        _kv_kernel,
        grid=(n_prompt // tb,),
        in_specs=[pl.BlockSpec((tb, MLA_KV_RANK), lambda i: (i, 0)), pl.BlockSpec((tb, LANES), lambda i: (i, 0)),
                  _const_spec(wuk2.shape), _const_spec(wuv2.shape)],
        out_specs=[pl.BlockSpec((tb, MLA_HEADS * MLA_QHEAD_PAD), lambda i: (i, 0)),
                   pl.BlockSpec((tb, MLA_V_W), lambda i: (i, 0))],
        out_shape=[jax.ShapeDtypeStruct((n_prompt, MLA_HEADS * MLA_QHEAD_PAD), BF16),
                   jax.ShapeDtypeStruct((n_prompt, MLA_V_W), BF16)],
        compiler_params=_params("parallel"),
    )(ckv, kpe128, wuk2, wuv2)


def _mla_prompt_kernel(q_ref, k_ref, v_ref, o_ref, m_scr, l_scr, acc_scr):
    blk = ATTN_BLOCK
    i = pl.program_id(2)
    m_scr[...] = jnp.full(m_scr.shape, -jnp.inf, F32)
    l_scr[...] = jnp.zeros(l_scr.shape, F32)
    acc_scr[...] = jnp.zeros(acc_scr.shape, F32)

    def step(kb, masked):
        r0 = pl.multiple_of(kb * blk, blk)
        for g in range(ATTN_HEADS):
            qa, qb = g * MLA_QHEAD_PAD, (g + 1) * MLA_QHEAD_PAD
            va, vb = g * MLA_DV, (g + 1) * MLA_DV
            s = _dot_nt(q_ref[:, qa:qb], k_ref[pl.ds(r0, blk), qa:qb]) * MLA_SCALE
            if masked:
                row = lax.broadcasted_iota(jnp.int32, (blk, blk), 0)
                colm = lax.broadcasted_iota(jnp.int32, (blk, blk), 1)
                s = jnp.where(row >= colm, s, -jnp.inf)
            m_old = m_scr[g]
            m_new = jnp.maximum(m_old, jnp.max(s, axis=-1, keepdims=True))
            alpha = jnp.exp(m_old - m_new)
            p = jnp.exp(s - m_new)
            l_scr[g] = alpha * l_scr[g] + jnp.sum(p, axis=-1, keepdims=True)
            acc_scr[g] = alpha * acc_scr[g] + _dot(p.astype(BF16), v_ref[pl.ds(r0, blk), va:vb])
            m_scr[g] = m_new

    def body(kb, carry):
        step(kb, False)
        return carry

    lax.fori_loop(0, i, body, 0)
    step(i, True)
    for g in range(ATTN_HEADS):
        o_ref[:, g * MLA_DV:(g + 1) * MLA_DV] = acc_scr[g] / l_scr[g]


def _mla_prompt(qm, kfull, v, batch, seq):
    blk = ATTN_BLOCK
    nq = seq // blk
    hg = ATTN_HEADS
    return pl.pallas_call(
        _mla_prompt_kernel,
        grid=(batch, MLA_HEADS // hg, nq),
        in_specs=[pl.BlockSpec((blk, hg * MLA_QHEAD_PAD), lambda b, h, i: (b * nq + i, h)),
                  pl.BlockSpec((seq, hg * MLA_QHEAD_PAD), lambda b, h, i: (b, h)),
                  pl.BlockSpec((seq, hg * MLA_DV), lambda b, h, i: (b, h))],
        out_specs=pl.BlockSpec((blk, hg * MLA_DV), lambda b, h, i: (b * nq + i, h)),
        out_shape=jax.ShapeDtypeStruct((batch * seq, MLA_V_W), F32),
        scratch_shapes=[pltpu.VMEM((hg, blk, 1), F32), pltpu.VMEM((hg, blk, 1), F32),
                        pltpu.VMEM((hg, blk, MLA_DV), F32)],
        compiler_params=_params("parallel", "parallel", "arbitrary"),
    )(qm, kfull, v)


def _qlat_kernel(q_ref, wuk_ref, o_ref):
    q = q_ref[...]
    qlat = _dot_nt(q[:, :MLA_NOPE], wuk_ref[0])
    o_ref[0] = jnp.concatenate([qlat, q[:, MLA_NOPE:MLA_NOPE + MLA_ROPE].astype(F32)], axis=-1).astype(BF16)


def _qlat(qm, wuk_h, n_prompt, n_sample):
    rb = n_prompt // n_sample
    width = MLA_KV_RANK + MLA_ROPE
    return pl.pallas_call(
        _qlat_kernel,
        grid=(MLA_HEADS,),
        in_specs=[pl.BlockSpec((n_sample, MLA_QHEAD_PAD), lambda h: (rb, h)),
                  pl.BlockSpec((1, MLA_KV_RANK, MLA_NOPE), lambda h: (h, 0, 0))],
        out_specs=pl.BlockSpec((1, n_sample, width), lambda h: (h, 0, 0)),
        out_shape=jax.ShapeDtypeStruct((MLA_HEADS, n_sample, width), BF16),
        compiler_params=_params("parallel"),
    )(qm, wuk_h)


def _decode_kernel(pt_ref, q_ref, cnew_ref, knew_ref, *refs):
    np_ = DEC_PAGES
    ckv_refs = refs[:np_]
    kpe_refs = refs[np_:2 * np_]
    o_ref = refs[2 * np_]
    m_scr, l_scr, acc_scr = refs[2 * np_ + 1:]
    c = pl.program_id(1)
    rank = MLA_KV_RANK

    @pl.when(c == 0)
    def _():
        m_scr[...] = jnp.full(m_scr.shape, -jnp.inf, F32)
        l_scr[...] = jnp.zeros(l_scr.shape, F32)
        acc_scr[...] = jnp.zeros(acc_scr.shape, F32)

    q = q_ref[0]
    q_lat, q_pe = q[:, :rank], q[:, rank:]
    s = jnp.concatenate(
        [_dot_nt(q_lat, ckv_refs[pg][0, 0].astype(BF16)) + _dot(q_pe, kpe_refs[pg][0, 0].astype(BF16))
         for pg in range(np_)], axis=-1) * MLA_SCALE
    m_old = m_scr[...]
    m_new = jnp.maximum(m_old, jnp.max(s, axis=-1, keepdims=True))
    alpha = jnp.exp(m_old - m_new)
    p = jnp.exp(s - m_new)
    l_scr[...] = alpha * l_scr[...] + jnp.sum(p, axis=-1, keepdims=True)
    pb = p.astype(BF16)
    pv = _dot(pb[:, :PAGE_SIZE], ckv_refs[0][0, 0].astype(BF16))
    for pg in range(1, np_):
        pv = pv + _dot(pb[:, pg * PAGE_SIZE:(pg + 1) * PAGE_SIZE], ckv_refs[pg][0, 0].astype(BF16))
    acc_scr[...] = alpha * acc_scr[...] + pv
    m_scr[...] = m_new

    @pl.when(c == pl.num_programs(1) - 1)
    def _():
        cn = cnew_ref[0].astype(BF16).astype(F32)
        kn = knew_ref[0][:, :MLA_ROPE].astype(BF16).astype(F32)
        qf = q.astype(F32)
        s_new = (jnp.sum(qf[:, :rank] * cn, axis=-1, keepdims=True)
                 + jnp.sum(qf[:, rank:] * kn, axis=-1, keepdims=True)) * MLA_SCALE
        m_old2 = m_scr[...]
        m_fin = jnp.maximum(m_old2, s_new)
        a2 = jnp.exp(m_old2 - m_fin)
        p_new = jnp.exp(s_new - m_fin)
        l_fin = a2 * l_scr[...] + p_new
        acc = a2 * acc_scr[...] + p_new.astype(BF16).astype(F32) * cn
        o_ref[0] = acc / l_fin


def _decode(page_table, qfull, ckv_new, kpe_new, cache_ckv, cache_kpe_t):
    n_seq, n_pages = page_table.shape
    np_ = DEC_PAGES
    width = MLA_KV_RANK + MLA_ROPE

    def page_spec(rows, last, p):
        return pl.BlockSpec((1, 1, rows, last), lambda s, c, pt: (0, pt[s, c * np_ + p], 0, 0))

    grid_spec = pltpu.PrefetchScalarGridSpec(
        num_scalar_prefetch=1,
        grid=(n_seq, n_pages // np_),
        in_specs=[pl.BlockSpec((1, MLA_HEADS, width), lambda s, c, pt: (s, 0, 0)),
                  pl.BlockSpec((1, 1, MLA_KV_RANK), lambda s, c, pt: (s, 0, 0)),
                  pl.BlockSpec((1, 1, LANES), lambda s, c, pt: (s, 0, 0))]
                 + [page_spec(PAGE_SIZE, MLA_KV_RANK, p) for p in range(np_)]
                 + [page_spec(MLA_ROPE, PAGE_SIZE, p) for p in range(np_)],
        out_specs=pl.BlockSpec((1, MLA_HEADS, MLA_KV_RANK), lambda s, c, pt: (s, 0, 0)),
        scratch_shapes=[pltpu.VMEM((MLA_HEADS, 1), F32), pltpu.VMEM((MLA_HEADS, 1), F32),
                        pltpu.VMEM((MLA_HEADS, MLA_KV_RANK), F32)],
    )
    return pl.pallas_call(
        _decode_kernel,
        grid_spec=grid_spec,
        out_shape=jax.ShapeDtypeStruct((n_seq, MLA_HEADS, MLA_KV_RANK), F32),
        compiler_params=_params("parallel", "arbitrary"),
    )(page_table, qfull, ckv_new, kpe_new, *([cache_ckv] * np_), *([cache_kpe_t] * np_))


def _oup_kernel(o_ref, wuv_ref, out_ref):
    out_ref[...] = _dot(o_ref[0].astype(BF16), wuv_ref[0])


def _o_up(olat_h, wuv_h):
    n_sample = olat_h.shape[1]
    return pl.pallas_call(
        _oup_kernel,
        grid=(MLA_HEADS,),
        in_specs=[pl.BlockSpec((1, n_sample, MLA_KV_RANK), lambda h: (h, 0, 0)),
                  pl.BlockSpec((1, MLA_KV_RANK, MLA_DV), lambda h: (h, 0, 0))],
        out_specs=pl.BlockSpec((n_sample, MLA_DV), lambda h: (0, h)),
        out_shape=jax.ShapeDtypeStruct((n_sample, MLA_V_W), F32),
        compiler_params=_params("parallel"),
    )(olat_h, wuv_h)


def _gla_prompt_kernel(q_ref, k_ref, v_ref, g_ref, o_ref, sout_ref, st_scr):
    sub = GLA_SUB
    c = pl.program_id(1)

    @pl.when(c == 0)
    def _():
        st_scr[...] = jnp.zeros(st_scr.shape, F32)

    row = lax.broadcasted_iota(jnp.int32, (sub, sub), 0)
    colm = lax.broadcasted_iota(jnp.int32, (sub, sub), 1)
    tri = (row >= colm).astype(F32)
    rowi = lax.broadcasted_iota(jnp.int32, (sub, 1), 0)

    def chunk(ci, carry):
        r0 = pl.multiple_of(ci * sub, sub)
        b_all = jnp.dot(tri, g_ref[pl.ds(r0, sub), :], precision=lax.Precision.HIGHEST,
                        preferred_element_type=F32)
        for hd in range(GLA_HEADS):
            ka, kb = hd * GLA_DK, (hd + 1) * GLA_DK
            va, vb = hd * GLA_DV, (hd + 1) * GLA_DV
            q = q_ref[pl.ds(r0, sub), ka:kb]
            k = k_ref[pl.ds(r0, sub), ka:kb]
            v = v_ref[pl.ds(r0, sub), va:vb]
            b = b_all[:, ka:kb]
            st = st_scr[hd]
            o = _dot_nt((q * jnp.exp(b)).astype(BF16), st.astype(BF16))
            for j in range(sub):
                w = q * k[j:j + 1, :] * jnp.exp(jnp.minimum(b - b[j:j + 1, :], 0.0))
                a = jnp.sum(w, axis=-1, keepdims=True)
                o = o + jnp.where(rowi >= j, a, 0.0) * v[j:j + 1, :]
            o_ref[pl.ds(r0, sub), va:vb] = o
            bl = b[sub - 1:sub, :]
            kt = (k * jnp.exp(bl - b)).astype(BF16)
            st_scr[hd] = st * jnp.exp(bl) + _dot_tn(v.astype(BF16), kt)
        return carry

    lax.fori_loop(0, GLA_TOKENS // sub, chunk, 0)

    @pl.when(c == pl.num_programs(1) - 1)
    def _():
        for hd in range(GLA_HEADS):
            sout_ref[0, hd] = st_scr[hd].T


def _gla_prompt(gq, gk, gv, gg, batch, seq):
    tc = GLA_TOKENS
    nc = seq // tc

    def rows(w):
        return pl.BlockSpec((tc, w), lambda b, c: (b * nc + c, 0))

    return pl.pallas_call(
        _gla_prompt_kernel,
        grid=(batch, nc),
        in_specs=[rows(GLA_QK_W), rows(GLA_QK_W), rows(GLA_V_W), rows(GLA_QK_W)],
        out_specs=[rows(GLA_V_W), pl.BlockSpec((1, GLA_HEADS, GLA_DK, GLA_DV), lambda b, c: (b, 0, 0, 0))],
        out_shape=[jax.ShapeDtypeStruct((batch * seq, GLA_V_W), F32),
                   jax.ShapeDtypeStruct((batch, GLA_HEADS, GLA_DK, GLA_DV), F32)],
        scratch_shapes=[pltpu.VMEM((GLA_HEADS, GLA_DV, GLA_DK), F32)],
        compiler_params=_params("parallel", "arbitrary"),
    )(gq, gk, gv, gg)


def _gla_sample_kernel(qt_ref, kt_ref, gt_ref, v_ref, s_ref, snew_ref, o_ref):
    c = pl.program_id(1)
    n_seq = s_ref.shape[0]

    @pl.when(c == 0)
    def _():
        o_ref[...] = jnp.zeros(o_ref.shape, F32)

    qt = qt_ref[...]
    kt = kt_ref[...]
    at = jnp.exp(gt_ref[...])
    for s in range(n_seq):
        sn = at[:, s:s + 1] * s_ref[s, 0] + kt[:, s:s + 1] * v_ref[s:s + 1, :]
        snew_ref[s, 0] = sn
        o_ref[s:s + 1, :] += jnp.sum(qt[:, s:s + 1] * sn, axis=0, keepdims=True)


def _gla_sample(qt, kt, gt, v, state):
    n_seq = state.shape[0]
    rb = GLA_S_DK_BLOCK
    nb = GLA_DK // rb
    vec = pl.BlockSpec((rb, n_seq), lambda h, c: (h * nb + c, 0))
    st = pl.BlockSpec((n_seq, 1, rb, GLA_DV), lambda h, c: (0, h, c, 0))
    return pl.pallas_call(
        _gla_sample_kernel,
        grid=(GLA_HEADS, nb),
        in_specs=[vec, vec, vec, pl.BlockSpec((n_seq, GLA_DV), lambda h, c: (0, h)), st],
        out_specs=[st, pl.BlockSpec((n_seq, GLA_DV), lambda h, c: (0, h))],
        out_shape=[jax.ShapeDtypeStruct(state.shape, F32), jax.ShapeDtypeStruct((n_seq, GLA_V_W), F32)],
        compiler_params=_params("parallel", "arbitrary"),
    )(qt, kt, gt, v, state)


def _out_kernel(x_ref, go_ref, gr_ref, mo_ref, gn_ref, wout_ref, o_ref):
    gn = gn_ref[...]
    gr = gr_ref[...]
    acc = x_ref[...] + _dot(mo_ref[...].astype(BF16), wout_ref[GLA_V_W:, :])
    parts = []
    for hd in range(GLA_HEADS):
        a, b = hd * GLA_DV, (hd + 1) * GLA_DV
        parts.append(_rms(go_ref[:, a:b], gn) * jax.nn.silu(gr[:, a:b]))
    go = jnp.concatenate(parts, axis=-1).astype(BF16)
    o_ref[...] = acc + _dot(go, wout_ref[:GLA_V_W, :])


def _out_proj(x_all, gla_o, gr, mla_o, gn, wout):
    n = x_all.shape[0]
    tb = OUT_TOKENS

    def rows(w):
        return pl.BlockSpec((tb, w), lambda i: (i, 0))

    return pl.pallas_call(
        _out_kernel,
        grid=(n // tb,),
        in_specs=[rows(D_MODEL), rows(GLA_V_W), rows(GLA_V_W), rows(MLA_V_W), _const_spec(gn.shape),
                  _const_spec(wout.shape)],
        out_specs=rows(D_MODEL),
        out_shape=jax.ShapeDtypeStruct((n, D_MODEL), F32),
        compiler_params=_params("parallel"),
    )(x_all, gla_o, gr, mla_o, gn, wout)


def _topk_rows(x, ids, k):
    vals, poss = [], []
    for _ in range(k):
        m = jnp.max(x, axis=0, keepdims=True)
        p = jnp.min(jnp.where(x == m, ids, jnp.inf), axis=0, keepdims=True)
        x = jnp.where(ids == p, -jnp.inf, x)
        vals.append(m)
        poss.append(p)
    return jnp.concatenate(vals, axis=0), jnp.concatenate(poss, axis=0)


def _row_ids(n_rows, n_lanes, start, step):
    return (lax.broadcasted_iota(jnp.int32, (n_rows, n_lanes), 0) * step + start).astype(F32)


def _pair_candidates(s1, s2):
    k = PEER_TOPK
    n = s1.shape[1]
    lo = lax.broadcasted_iota(jnp.int32, (k, n), 0) < 4
    vals = [s1[0:1, :] + s2]
    ids = [_row_ids(k, n, 0, 1)]
    for r in (1, 2, 3):
        vals.append(s1[r:r + 1, :] + s2[0:8, :])
        ids.append(_row_ids(8, n, r * k, 1))
    vals.append(jnp.where(lo, -jnp.inf, s1 + s2[0:1, :]))
    ids.append(_row_ids(k, n, 0, k))
    for c in (1, 2):
        vals.append(jnp.where(lo[0:8], -jnp.inf, s1[0:8, :] + s2[c:c + 1, :]))
        ids.append(_row_ids(8, n, c, k))
    return jnp.concatenate(vals, axis=0), jnp.concatenate(ids, axis=0)


def _peerq_kernel(x_ref, gffn_ref, wq_ref, keys_ref, xn_ref, a_ref, b_ref, g_ref,
                  q_scr, v_scr, i_scr, sa_scr, sb_scr, sg_scr):
    topk = PEER_TOPK
    half = PEER_DKEY // 2
    xn = _rms(x_ref[...], gffn_ref[...]).astype(BF16)
    xn_ref[...] = xn
    q = _dot(xn, wq_ref[...])
    for hp in range(2 * PEER_HEADS):
        q_scr[hp] = q[:, hp * half:(hp + 1) * half].astype(BF16)

    key_ids = _row_ids(PEER_NKEYS, LANES, 0, 1)

    def side(hd, carry):
        for hp in (2 * hd, 2 * hd + 1):
            s = _dot_nt(keys_ref[hp], q_scr[hp])
            vals, poss = _topk_rows(s, key_ids, topk)
            v_scr[hp] = vals
            i_scr[hp] = poss
        return carry

    lax.fori_loop(0, PEER_HEADS, side, 0)

    r16 = _row_ids(topk, LANES, 0, 1)

    def head(hd, carry):
        s1, s2 = v_scr[2 * hd], v_scr[2 * hd + 1]
        i1, i2 = i_scr[2 * hd], i_scr[2 * hd + 1]
        cand, cand_ids = _pair_candidates(s1, s2)
        top_s, pos = _topk_rows(cand, cand_ids, topk)
        e = jnp.exp(top_s - top_s[0:1, :])
        gate = e / jnp.sum(e, axis=0, keepdims=True)
        r = jnp.floor(pos * (1.0 / topk))
        cc = pos - r * topk
        a_rows, b_rows = [], []
        for kk in range(topk):
            a_rows.append(jnp.sum(jnp.where(r16 == r[kk:kk + 1, :], i1, 0.0), axis=0, keepdims=True))
            b_rows.append(jnp.sum(jnp.where(r16 == cc[kk:kk + 1, :], i2, 0.0), axis=0, keepdims=True))
        base = pl.multiple_of(hd * topk, topk)
        sa_scr[pl.ds(base, topk), :] = jnp.concatenate(a_rows, axis=0)
        sb_scr[pl.ds(base, topk), :] = jnp.concatenate(b_rows, axis=0)
        sg_scr[pl.ds(base, topk), :] = gate
        return carry

    lax.fori_loop(0, PEER_HEADS, head, 0)
    a_ref[...] = sa_scr[...].T
    b_ref[...] = sb_scr[...].T
    g_ref[...] = sg_scr[...].T


def _peer_query(x2, gffn, wq, keys2):
    n = x2.shape[0]
    tb = PEERQ_TOKENS
    nslot = PEER_HEADS * PEER_TOPK
    half = PEER_DKEY // 2

    def rows(w):
        return pl.BlockSpec((tb, w), lambda i: (i, 0))

    return pl.pallas_call(
        _peerq_kernel,
        grid=(n // tb,),
        in_specs=[rows(D_MODEL), _const_spec(gffn.shape), _const_spec(wq.shape), _const_spec(keys2.shape)],
        out_specs=[rows(D_MODEL), rows(nslot), rows(nslot), rows(nslot)],
        out_shape=[jax.ShapeDtypeStruct((n, D_MODEL), BF16)] + [jax.ShapeDtypeStruct((n, nslot), F32)] * 3,
        scratch_shapes=[pltpu.VMEM((2 * PEER_HEADS, tb, half), BF16),
                        pltpu.VMEM((2 * PEER_HEADS, PEER_TOPK, tb), F32),
                        pltpu.VMEM((2 * PEER_HEADS, PEER_TOPK, tb), F32),
                        pltpu.VMEM((nslot, tb), F32), pltpu.VMEM((nslot, tb), F32), pltpu.VMEM((nslot, tb), F32)],
        compiler_params=_params("parallel"),
    )(x2, gffn, wq, keys2)


def _peer_dense_kernel(xn_ref, a_ref, b_ref, g_ref, ut_ref, v_ref, x2_ref, gfin_ref, out_ref, gp_scr):
    tb = PEER_TOKENS
    te = PEER_EXPERTS
    stride = tb + PEER_G_PAD
    j = pl.program_id(1)

    @pl.when(j == 0)
    def _():
        out_ref[...] = jnp.zeros(out_ref.shape, F32)
        sub_iota = lax.broadcasted_iota(jnp.int32, (PEER_NKEYS, LANES), 0).astype(F32)

        def tok(t8, carry):
            for u in range(PEER_G_UNROLL):
                t = t8 * PEER_G_UNROLL + u
                arow = a_ref[pl.ds(t, 1), :]
                brow = b_ref[pl.ds(t, 1), :]
                grow = g_ref[pl.ds(t, 1), :]
                at = jnp.where(sub_iota == arow, 1.0, 0.0).astype(BF16)
                bt = jnp.where(sub_iota == brow, grow, 0.0).astype(BF16)
                gp_scr[pl.ds(t, PEER_NKEYS, stride=stride), :] = _dot_nt(at, bt)
            return carry

        lax.fori_loop(0, tb // PEER_G_UNROLL, tok, 0)

    act = _gelu(_dot(xn_ref[...], ut_ref[0]))
    n_a = te // LANES
    parts = []
    for jj in range(n_a):
        r0 = pl.multiple_of((j * n_a + jj) * stride, 8)
        parts.append(gp_scr[pl.ds(r0, tb), :] * act[:, jj * LANES:(jj + 1) * LANES])
    p = jnp.concatenate(parts, axis=-1).astype(BF16)
    out_ref[...] += _dot(p, v_ref[...])

    @pl.when(j == pl.num_programs(1) - 1)
    def _():
        out_ref[...] = _rms(x2_ref[...] + out_ref[...], gfin_ref[...])


def _peer_dense(xn, sa, sb, sg, ut, vv, x2, gfin):
    n = xn.shape[0]
    tb = PEER_TOKENS
    te = PEER_EXPERTS
    n_exp = vv.shape[0]
    nslot = PEER_HEADS * PEER_TOPK

    def rows(w):
        return pl.BlockSpec((tb, w), lambda i, j: (i, 0))

    return pl.pallas_call(
        _peer_dense_kernel,
        grid=(n // tb, n_exp // te),
        in_specs=[rows(D_MODEL), rows(nslot), rows(nslot), rows(nslot),
                  pl.BlockSpec((1, D_MODEL, te), lambda i, j: (j, 0, 0)),
                  pl.BlockSpec((te, D_MODEL), lambda i, j: (j, 0)),
                  rows(D_MODEL), _const_spec(gfin.shape)],
        out_specs=rows(D_MODEL),
        out_shape=jax.ShapeDtypeStruct((n, D_MODEL), F32),
        scratch_shapes=[pltpu.VMEM((PEER_NKEYS * (tb + PEER_G_PAD), LANES), F32)],
        compiler_params=_params("parallel", "arbitrary"),
    )(xn, sa, sb, sg, ut, vv, x2, gfin)


def _rope_tables(pos):
    half = MLA_ROPE // 2
    inv = ROPE_THETA ** (-jnp.arange(half, dtype=F32) / half)
    ang = pos.astype(F32)[:, None] * inv[None, :]
    cos, sin = jnp.cos(ang), jnp.sin(ang)
    n = pos.shape[0]
    cc = jnp.concatenate([cos, cos], axis=-1)
    ss = jnp.concatenate([-sin, sin], axis=-1)
    z64 = jnp.zeros((n, LANES - MLA_ROPE), F32)
    cq_tab = jnp.concatenate([jnp.ones((n, MLA_NOPE), F32), cc, z64], axis=-1)
    sq_tab = jnp.concatenate([jnp.zeros((n, MLA_NOPE), F32), ss, z64], axis=-1)
    ck_tab = jnp.concatenate([cc, z64], axis=-1)
    sk_tab = jnp.concatenate([ss, z64], axis=-1)
    return cq_tab, sq_tab, ck_tab, sk_tab


def _prep_win(w_in):
    d = w_in.shape[0]
    o = 0
    pieces = {}
    for name, width in (("gq", GLA_QK_W), ("gk", GLA_QK_W), ("gv", GLA_V_W), ("gr", GLA_V_W), ("ga", GLA_LOWRANK),
                        ("cq", MLA_Q_RANK), ("ckv", MLA_KV_RANK), ("kpe", MLA_ROPE)):
        pieces[name] = w_in[:, o:o + width]
        o += width
    half = MLA_ROPE // 2
    kpe = pieces["kpe"]
    ksw = jnp.concatenate([kpe[:, half:], kpe[:, :half]], axis=-1)
    z = lambda w: jnp.zeros((d, w), w_in.dtype)
    cols = [pieces["gq"], pieces["gk"], pieces["gv"], pieces["gr"], pieces["cq"], pieces["ckv"],
            kpe, pieces["ga"], z(LANES - MLA_ROPE - GLA_LOWRANK), ksw, z(LANES - MLA_ROPE)]
    return jnp.concatenate(cols, axis=-1).astype(BF16)


def _prep_wuq(w_uq):
    r = w_uq.shape[0]
    w = w_uq.reshape(r, MLA_HEADS, MLA_NOPE + MLA_ROPE)
    half = MLA_ROPE // 2
    nope, pe = w[..., :MLA_NOPE], w[..., MLA_NOPE:]
    pad = jnp.zeros((r, MLA_HEADS, MLA_QHEAD_PAD - MLA_NOPE - MLA_ROPE), w_uq.dtype)
    plain = jnp.concatenate([nope, pe, pad], axis=-1)
    swapped = jnp.concatenate([jnp.zeros_like(nope), pe[..., half:], pe[..., :half], pad], axis=-1)
    width = MLA_HEADS * MLA_QHEAD_PAD
    return plain.reshape(r, width).astype(BF16), swapped.reshape(r, width).astype(BF16)


def kernel(x_prompt, x_sample, cache_ckv, cache_kpe, state_gla, page_table, norm_attn, w_in, gla_w_alpha, gla_b_alpha, gla_norm, mla_q_norm, mla_w_uq, mla_kv_norm, mla_w_uk, mla_w_uv, w_out, norm_ffn, peer_w_q, peer_keys, peer_u, peer_v, norm_final):
    batch, seq, d = x_prompt.shape
    n_seq, dec_seq, _ = x_sample.shape
    depth = w_in.shape[0]
    assert depth == 1 and dec_seq == 1 and d == D_MODEL
    n_prompt = batch * seq
    n = n_prompt + n_seq
    past_len = page_table.shape[1] * PAGE_SIZE

    x_all = jnp.concatenate([x_prompt.reshape(n_prompt, d), x_sample.reshape(n_seq, d)], axis=0)
    pos = jnp.concatenate([jnp.tile(jnp.arange(seq, dtype=jnp.int32), batch),
                           jnp.full((n_seq,), past_len, jnp.int32)])
    tabs = _rope_tables(pos)

    row = lambda g: g.reshape(1, -1).astype(F32)
    win_p = _prep_win(w_in[0])
    walpha_p = jnp.zeros((LANES, GLA_QK_W), F32).at[MLA_ROPE:MLA_ROPE + GLA_LOWRANK].set(gla_w_alpha[0]).astype(BF16)
    wuq_p, wuqs_p = _prep_wuq(mla_w_uq[0])

    gq, gk, gv, gr, gg, qm, ckv, kpe128 = _proj(
        x_all, tabs, row(norm_attn[0]), win_p, walpha_p, row(gla_b_alpha[0]), row(mla_q_norm[0]),
        wuq_p, wuqs_p, row(mla_kv_norm[0]))

    wuk2 = mla_w_uk[0].reshape(MLA_KV_RANK, MLA_HEADS * MLA_NOPE).astype(BF16)
    wuv2 = mla_w_uv[0].reshape(MLA_KV_RANK, MLA_V_W).astype(BF16)
    kfull, vfull = _kv_up(ckv, kpe128, wuk2, wuv2, n_prompt)
    mla_o_p = _mla_prompt(qm, kfull, vfull, batch, seq)

    wuk_h = jnp.transpose(mla_w_uk[0], (1, 0, 2)).astype(BF16)
    wuv_h = jnp.transpose(mla_w_uv[0], (1, 0, 2)).astype(BF16)
    qfull = jnp.transpose(_qlat(qm, wuk_h, n_prompt, n_seq), (1, 0, 2))
    olat = _decode(page_table, qfull, ckv[n_prompt:].reshape(n_seq, 1, MLA_KV_RANK),
                   kpe128[n_prompt:].reshape(n_seq, 1, LANES), cache_ckv, jnp.swapaxes(cache_kpe, 2, 3))
    mla_o_s = _o_up(jnp.transpose(olat, (1, 0, 2)), wuv_h)

    gla_o_p, gla_state_p = _gla_prompt(gq, gk, gv, gg, batch, seq)
    gla_state_s, gla_o_s = _gla_sample(gq[n_prompt:].T, gk[n_prompt:].T, gg[n_prompt:].T, gv[n_prompt:],
                                       state_gla[0])

    gla_o = jnp.concatenate([gla_o_p, gla_o_s], axis=0)
    mla_o = jnp.concatenate([mla_o_p, mla_o_s], axis=0)
    x2 = _out_proj(x_all, gla_o, gr, mla_o, row(gla_norm[0]), w_out[0].astype(BF16))

    keys2 = peer_keys[0].reshape(2 * PEER_HEADS, PEER_NKEYS, PEER_DKEY // 2).astype(BF16)
    xn, sa, sb, sg = _peer_query(x2, row(norm_ffn[0]), peer_w_q[0].astype(BF16), keys2)
    ut = jnp.transpose(peer_u[0].astype(BF16).reshape(-1, PEER_EXPERTS, d), (0, 2, 1))
    y = _peer_dense(xn, sa, sb, sg, ut, peer_v[0].astype(BF16), x2, row(norm_final))

    y_prompt = y[:n_prompt].reshape(batch, seq, d)
    y_sample = y[n_prompt:].reshape(n_seq, dec_seq, d)
    ckv_p = ckv[:n_prompt].reshape(1, batch, seq, MLA_KV_RANK)
    kpe_p = kpe128[:n_prompt, :MLA_ROPE].reshape(1, batch, seq, MLA_ROPE)
    ckv_s = ckv[n_prompt:].reshape(1, n_seq, dec_seq, MLA_KV_RANK)
    kpe_s = kpe128[n_prompt:, :MLA_ROPE].reshape(1, n_seq, dec_seq, MLA_ROPE)
    return (y_prompt, y_sample, ckv_p, kpe_p, gla_state_p[None], ckv_s, kpe_s, gla_state_s[None])
```

```python
import functools

import jax
import jax.numpy as jnp
from jax import lax
from jax.experimental import pallas as pl
from jax.experimental.pallas import tpu as pltpu

F32 = jnp.float32
BF16 = jnp.bfloat16

D_MODEL = 2048
PAGE_SIZE = 128
GLA_HEADS = 4
GLA_DK = 128
GLA_DV = 256
GLA_LOWRANK = 16
GLA_TAU = 16.0
MLA_HEADS = 8
MLA_Q_RANK = 512
MLA_KV_RANK = 512
MLA_NOPE = 128
MLA_ROPE = 64
MLA_DV = 128
MLA_SCALE = (MLA_NOPE + MLA_ROPE) ** -0.5
ROPE_THETA = 10000.0
PEER_HEADS = 8
PEER_NKEYS = 128
PEER_DKEY = 256
PEER_TOPK = 16
EPS = 1e-6

GLA_QK_W = GLA_HEADS * GLA_DK
GLA_V_W = GLA_HEADS * GLA_DV
MLA_V_W = MLA_HEADS * MLA_DV
MLA_QHEAD_PAD = 256
LANES = 128

VMEM_LIMIT_BYTES = 56 * 1024 * 1024

PROJ_TOKENS = 320
KV_TOKENS = 512
ATTN_BLOCK = 512
ATTN_HEADS = 4
GLA_TOKENS = 256
GLA_SUB = 16
DEC_PAGES = 32
GLA_S_DK_BLOCK = 32
OUT_TOKENS = 320
PEERQ_TOKENS = 128
PEERQ_SIDES = 4
PEERQ_HEADS = 2
PEER_TOKENS = 320
PEER_EXPERTS = 512
PEER_G_PAD = 8
PEER_G_UNROLL = 8


def _params(*sem):
    return pltpu.CompilerParams(dimension_semantics=sem, vmem_limit_bytes=VMEM_LIMIT_BYTES)


def _const_spec(shape):
    nd = len(shape)
    return pl.BlockSpec(shape, lambda *_: (0,) * nd, pipeline_mode=pl.Buffered(1))


def _rms(x, g):
    return x * lax.rsqrt(jnp.mean(x * x, axis=-1, keepdims=True) + EPS) * g


def _gelu(x):
    return 0.5 * x * (1.0 + lax.erf(x * (0.5 ** 0.5)))


def _dot(a, b):
    return jnp.dot(a, b, preferred_element_type=F32)


def _dot_nt(a, b):
    return lax.dot_general(a, b, (((1,), (1,)), ((), ())), preferred_element_type=F32)


def _dot_tn(a, b):
    return lax.dot_general(a, b, (((0,), (0,)), ((), ())), preferred_element_type=F32)


_C_GQ, _C_GK, _C_GV, _C_GR = 0, 512, 1024, 2048
_C_CQ, _C_CKV, _C_KPE, _C_KSW, _C_END = 3072, 3584, 4096, 4224, 4352


def _proj_kernel(x_ref, gattn_ref, win_ref, walpha_ref, balpha_ref, gqn_ref, wuq_ref, wuqs_ref, gkv_ref,
                 cq_ref, sq_ref, ck_ref, sk_ref,
                 oq_ref, ok_ref, ov_ref, or_ref, og_ref, oqm_ref, ockv_ref, okpe_ref):
    h = _rms(x_ref[...], gattn_ref[...]).astype(BF16)

    def col(a, b):
        return _dot(h, win_ref[:, a:b])

    oq_ref[...] = col(_C_GQ, _C_GK) * (GLA_DK ** -0.5)
    ok_ref[...] = col(_C_GK, _C_GV)
    ov_ref[...] = col(_C_GV, _C_GR)
    or_ref[...] = col(_C_GR, _C_CQ)
    grp = col(_C_KPE, _C_KSW)
    swp = col(_C_KSW, _C_END)
    xa = _dot(grp.astype(BF16), walpha_ref[...]) + balpha_ref[...]
    og_ref[...] = jax.nn.log_sigmoid(xa) / GLA_TAU
    okpe_ref[...] = grp * ck_ref[...] + swp * sk_ref[...]
    ockv_ref[...] = _rms(col(_C_CKV, _C_KPE), gkv_ref[...])
    cqn = _rms(col(_C_CQ, _C_CKV), gqn_ref[...]).astype(BF16)
    cq_tab = cq_ref[...]
    sq_tab = sq_ref[...]
    for hd in range(MLA_HEADS):
        a, b = hd * MLA_QHEAD_PAD, (hd + 1) * MLA_QHEAD_PAD
        raw = _dot(cqn, wuq_ref[:, a:b])
        sw = _dot(cqn, wuqs_ref[:, a:b])
        oqm_ref[:, a:b] = (raw * cq_tab + sw * sq_tab).astype(BF16)


def _proj(x_all, tabs, gattn, win_p, walpha_p, balpha, gqn, wuq_p, wuqs_p, gkv):
    n = x_all.shape[0]
    tb = PROJ_TOKENS
    cq_tab, sq_tab, ck_tab, sk_tab = tabs

    def rows(w):
        return pl.BlockSpec((tb, w), lambda i: (i, 0))

    out_w = (GLA_QK_W, GLA_QK_W, GLA_V_W, GLA_V_W, GLA_QK_W, MLA_HEADS * MLA_QHEAD_PAD, MLA_KV_RANK, LANES)
    out_dt = (F32, F32, F32, F32, F32, BF16, F32, F32)
    return pl.pallas_call(
        _proj_kernel,
        grid=(n // tb,),
        in_specs=[rows(D_MODEL), _const_spec(gattn.shape), _const_spec(win_p.shape), _const_spec(walpha_p.shape),
                  _const_spec(balpha.shape), _const_spec(gqn.shape), _const_spec(wuq_p.shape),
                  _const_spec(wuqs_p.shape), _const_spec(gkv.shape),
                  rows(MLA_QHEAD_PAD), rows(MLA_QHEAD_PAD), rows(LANES), rows(LANES)],
        out_specs=[rows(w) for w in out_w],
        out_shape=[jax.ShapeDtypeStruct((n, w), dt) for w, dt in zip(out_w, out_dt)],
        compiler_params=_params("parallel"),
    )(x_all, gattn, win_p, walpha_p, balpha, gqn, wuq_p, wuqs_p, gkv, cq_tab, sq_tab, ck_tab, sk_tab)


def _kv_kernel(ckv_ref, kpe_ref, wuk_ref, wuv_ref, k_ref, v_ref):
    c = ckv_ref[...].astype(BF16)
    kpe = kpe_ref[...].astype(BF16)
    kn = _dot(c, wuk_ref[...]).astype(BF16)
    v_ref[...] = _dot(c, wuv_ref[...]).astype(BF16)
    for hd in range(MLA_HEADS):
        a = hd * MLA_QHEAD_PAD
        k_ref[:, a:a + MLA_NOPE] = kn[:, hd * MLA_NOPE:(hd + 1) * MLA_NOPE]
        k_ref[:, a + MLA_NOPE:a + MLA_QHEAD_PAD] = kpe


def _kv_up(ckv, kpe128, wuk2, wuv2, n_prompt):
    tb = KV_TOKENS
    return pl.pallas_call(
        _kv_kernel,
        grid=(n_prompt // tb,),
        in_specs=[pl.BlockSpec((tb, MLA_KV_RANK), lambda i: (i, 0)), pl.BlockSpec((tb, LANES), lambda i: (i, 0)),
                  _const_spec(wuk2.shape), _const_spec(wuv2.shape)],
        out_specs=[pl.BlockSpec((tb, MLA_HEADS * MLA_QHEAD_PAD), lambda i: (i, 0)),
                   pl.BlockSpec((tb, MLA_V_W), lambda i: (i, 0))],
        out_shape=[jax.ShapeDtypeStruct((n_prompt, MLA_HEADS * MLA_QHEAD_PAD), BF16),
                   jax.ShapeDtypeStruct((n_prompt, MLA_V_W), BF16)],
        compiler_params=_params("parallel"),
    )(ckv, kpe128, wuk2, wuv2)


def _mla_prompt_kernel(q_ref, k_ref, v_ref, o_ref, m_scr, l_scr, acc_scr):
    blk = ATTN_BLOCK
    i = pl.program_id(2)
    m_scr[...] = jnp.full(m_scr.shape, -jnp.inf, F32)
    l_scr[...] = jnp.zeros(l_scr.shape, F32)
    acc_scr[...] = jnp.zeros(acc_scr.shape, F32)

    def step(kb, masked):
        r0 = pl.multiple_of(kb * blk, blk)

        def scores(g):
            qa, qb = g * MLA_QHEAD_PAD, (g + 1) * MLA_QHEAD_PAD
            s = _dot_nt(q_ref[:, qa:qb], k_ref[pl.ds(r0, blk), qa:qb]) * MLA_SCALE
            if masked:
                row = lax.broadcasted_iota(jnp.int32, (blk, blk), 0)
                colm = lax.broadcasted_iota(jnp.int32, (blk, blk), 1)
                s = jnp.where(row >= colm, s, -jnp.inf)
            return s

        def softmax_pv(g, s):
            m_old = m_scr[g]
            m_new = jnp.maximum(m_old, jnp.max(s, axis=-1, keepdims=True))
            alpha = jnp.exp(m_old - m_new)
            p = jnp.exp(s - m_new)
            l_scr[g] = alpha * l_scr[g] + jnp.sum(p, axis=-1, keepdims=True)
            acc_scr[g] = alpha * acc_scr[g] + _dot(p.astype(BF16), v_ref[pl.ds(r0, blk), g * MLA_DV:(g + 1) * MLA_DV])
            m_scr[g] = m_new

        s_next = scores(0)
        for g in range(ATTN_HEADS):
            s_cur = s_next
            if g + 1 < ATTN_HEADS:
                s_next = scores(g + 1)
            softmax_pv(g, s_cur)

    def body(kb, carry):
        step(kb, False)
        return carry

    lax.fori_loop(0, i, body, 0)
    step(i, True)
    for g in range(ATTN_HEADS):
        o_ref[:, g * MLA_DV:(g + 1) * MLA_DV] = acc_scr[g] / l_scr[g]


def _mla_prompt(qm, kfull, v, batch, seq):
    blk = ATTN_BLOCK
    nq = seq // blk
    hg = ATTN_HEADS
    return pl.pallas_call(
        _mla_prompt_kernel,
        grid=(batch, MLA_HEADS // hg, nq),
        in_specs=[pl.BlockSpec((blk, hg * MLA_QHEAD_PAD), lambda b, h, i: (b * nq + i, h)),
                  pl.BlockSpec((seq, hg * MLA_QHEAD_PAD), lambda b, h, i: (b, h)),
                  pl.BlockSpec((seq, hg * MLA_DV), lambda b, h, i: (b, h))],
        out_specs=pl.BlockSpec((blk, hg * MLA_DV), lambda b, h, i: (b * nq + i, h)),
        out_shape=jax.ShapeDtypeStruct((batch * seq, MLA_V_W), F32),
        scratch_shapes=[pltpu.VMEM((hg, blk, 1), F32), pltpu.VMEM((hg, blk, 1), F32),
                        pltpu.VMEM((hg, blk, MLA_DV), F32)],
        compiler_params=_params("parallel", "parallel", "arbitrary"),
    )(qm, kfull, v)


def _qlat_kernel(q_ref, wuk_ref, o_ref):
    q = q_ref[...]
    qlat = _dot_nt(q[:, :MLA_NOPE], wuk_ref[0])
    o_ref[0] = jnp.concatenate([qlat, q[:, MLA_NOPE:MLA_NOPE + MLA_ROPE].astype(F32)], axis=-1).astype(BF16)


def _qlat(qm, wuk_h, n_prompt, n_sample):
    rb = n_prompt // n_sample
    width = MLA_KV_RANK + MLA_ROPE
    return pl.pallas_call(
        _qlat_kernel,
        grid=(MLA_HEADS,),
        in_specs=[pl.BlockSpec((n_sample, MLA_QHEAD_PAD), lambda h: (rb, h)),
                  pl.BlockSpec((1, MLA_KV_RANK, MLA_NOPE), lambda h: (h, 0, 0))],
        out_specs=pl.BlockSpec((1, n_sample, width), lambda h: (h, 0, 0)),
        out_shape=jax.ShapeDtypeStruct((MLA_HEADS, n_sample, width), BF16),
        compiler_params=_params("parallel"),
    )(qm, wuk_h)


def _decode_kernel(pt_ref, q_ref, cnew_ref, knew_ref, ckv_hbm, kpe_hbm, o_ref,
                   cbuf, kbuf, sem, m_scr, l_scr, acc_scr):
    np_ = DEC_PAGES
    s_id = pl.program_id(0)
    c = pl.program_id(1)
    n_chunks = pl.num_programs(1)
    n_steps = pl.num_programs(0) * n_chunks
    step = s_id * n_chunks + c
    slot = step % 2
    nxt = jnp.minimum(step + 1, n_steps - 1)
    nxt_seq = nxt // n_chunks
    nxt_page0 = (nxt % n_chunks) * np_
    rank = MLA_KV_RANK

    def page_copies(seq, page0, pg, slot_):
        page = pt_ref[seq, page0 + pg]
        return (pltpu.make_async_copy(ckv_hbm.at[0, page], cbuf.at[slot_, pg], sem.at[0, slot_]),
                pltpu.make_async_copy(kpe_hbm.at[0, page], kbuf.at[slot_, pg], sem.at[1, slot_]))

    def wait_slot(slot_):
        pltpu.make_async_copy(ckv_hbm.at[0, pl.ds(0, np_)], cbuf.at[slot_], sem.at[0, slot_]).wait()
        pltpu.make_async_copy(kpe_hbm.at[0, pl.ds(0, np_)], kbuf.at[slot_], sem.at[1, slot_]).wait()

    @pl.when(step == 0)
    def _():
        for pg in range(np_):
            for cp in page_copies(0, 0, pg, 0):
                cp.start()

    @pl.when(c == 0)
    def _():
        m_scr[...] = jnp.full(m_scr.shape, -jnp.inf, F32)
        l_scr[...] = jnp.zeros(l_scr.shape, F32)
        acc_scr[...] = jnp.zeros(acc_scr.shape, F32)

    wait_slot(slot)
    q = q_ref[0]
    q_lat, q_pe = q[:, :rank], q[:, rank:]
    s_parts = []
    for pg in range(np_):
        for cp in page_copies(nxt_seq, nxt_page0, pg, 1 - slot):
            cp.start()
        s_parts.append(_dot_nt(q_lat, cbuf[slot, pg].astype(BF16)) + _dot(q_pe, kbuf[slot, pg].astype(BF16)))
    s = jnp.concatenate(s_parts, axis=-1) * MLA_SCALE
    m_old = m_scr[...]
    m_new = jnp.maximum(m_old, jnp.max(s, axis=-1, keepdims=True))
    alpha = jnp.exp(m_old - m_new)
    p = jnp.exp(s - m_new)
    l_scr[...] = alpha * l_scr[...] + jnp.sum(p, axis=-1, keepdims=True)
    pb = p.astype(BF16)
    pv = _dot(pb[:, :PAGE_SIZE], cbuf[slot, 0].astype(BF16))
    for pg in range(1, np_):
        pv = pv + _dot(pb[:, pg * PAGE_SIZE:(pg + 1) * PAGE_SIZE], cbuf[slot, pg].astype(BF16))
    acc_scr[...] = alpha * acc_scr[...] + pv
    m_scr[...] = m_new

    @pl.when(step == n_steps - 1)
    def _():
        wait_slot(1 - slot)

    @pl.when(c == pl.num_programs(1) - 1)
    def _():
        cn = cnew_ref[0].astype(BF16).astype(F32)
        kn = knew_ref[0][:, :MLA_ROPE].astype(BF16).astype(F32)
        qf = q.astype(F32)
        s_new = (jnp.sum(qf[:, :rank] * cn, axis=-1, keepdims=True)
                 + jnp.sum(qf[:, rank:] * kn, axis=-1, keepdims=True)) * MLA_SCALE
        m_old2 = m_scr[...]
        m_fin = jnp.maximum(m_old2, s_new)
        a2 = jnp.exp(m_old2 - m_fin)
        p_new = jnp.exp(s_new - m_fin)
        l_fin = a2 * l_scr[...] + p_new
        acc = a2 * acc_scr[...] + p_new.astype(BF16).astype(F32) * cn
        o_ref[0] = acc / l_fin


def _decode(page_table, qfull, ckv_new, kpe_new, cache_ckv, cache_kpe_t):
    n_seq, n_pages = page_table.shape
    np_ = DEC_PAGES
    width = MLA_KV_RANK + MLA_ROPE

    grid_spec = pltpu.PrefetchScalarGridSpec(
        num_scalar_prefetch=1,
        grid=(n_seq, n_pages // np_),
        in_specs=[pl.BlockSpec((1, MLA_HEADS, width), lambda s, c, pt: (s, 0, 0)),
                  pl.BlockSpec((1, 1, MLA_KV_RANK), lambda s, c, pt: (s, 0, 0)),
                  pl.BlockSpec((1, 1, LANES), lambda s, c, pt: (s, 0, 0)),
                  pl.BlockSpec(memory_space=pl.ANY), pl.BlockSpec(memory_space=pl.ANY)],
        out_specs=pl.BlockSpec((1, MLA_HEADS, MLA_KV_RANK), lambda s, c, pt: (s, 0, 0)),
        scratch_shapes=[pltpu.VMEM((2, np_, PAGE_SIZE, MLA_KV_RANK), F32),
                        pltpu.VMEM((2, np_, MLA_ROPE, PAGE_SIZE), F32),
                        pltpu.SemaphoreType.DMA((2, 2)),
                        pltpu.VMEM((MLA_HEADS, 1), F32), pltpu.VMEM((MLA_HEADS, 1), F32),
                        pltpu.VMEM((MLA_HEADS, MLA_KV_RANK), F32)],
    )
    return pl.pallas_call(
        _decode_kernel,
        grid_spec=grid_spec,
        out_shape=jax.ShapeDtypeStruct((n_seq, MLA_HEADS, MLA_KV_RANK), F32),
        compiler_params=_params("arbitrary", "arbitrary"),
    )(page_table, qfull, ckv_new, kpe_new, cache_ckv, cache_kpe_t)


def _oup_kernel(o_ref, wuv_ref, out_ref):
    out_ref[...] = _dot(o_ref[0].astype(BF16), wuv_ref[0])


def _o_up(olat_h, wuv_h):
    n_sample = olat_h.shape[1]
    return pl.pallas_call(
        _oup_kernel,
        grid=(MLA_HEADS,),
        in_specs=[pl.BlockSpec((1, n_sample, MLA_KV_RANK), lambda h: (h, 0, 0)),
                  pl.BlockSpec((1, MLA_KV_RANK, MLA_DV), lambda h: (h, 0, 0))],
        out_specs=pl.BlockSpec((n_sample, MLA_DV), lambda h: (0, h)),
        out_shape=jax.ShapeDtypeStruct((n_sample, MLA_V_W), F32),
        compiler_params=_params("parallel"),
    )(olat_h, wuv_h)


def _gla_prompt_kernel(q_ref, k_ref, v_ref, g_ref, o_ref, sout_ref, st_scr):
    sub = GLA_SUB
    c = pl.program_id(1)

    @pl.when(c == 0)
    def _():
        st_scr[...] = jnp.zeros(st_scr.shape, F32)

    row = lax.broadcasted_iota(jnp.int32, (sub, sub), 0)
    colm = lax.broadcasted_iota(jnp.int32, (sub, sub), 1)
    tri = (row >= colm).astype(F32)
    rowi = lax.broadcasted_iota(jnp.int32, (sub, 1), 0)

    def chunk(ci, carry):
        r0 = pl.multiple_of(ci * sub, sub)
        b_all = jnp.dot(tri, g_ref[pl.ds(r0, sub), :], precision=lax.Precision.HIGHEST,
                        preferred_element_type=F32)
        for hd in range(GLA_HEADS):
            ka, kb = hd * GLA_DK, (hd + 1) * GLA_DK
            va, vb = hd * GLA_DV, (hd + 1) * GLA_DV
            q = q_ref[pl.ds(r0, sub), ka:kb]
            k = k_ref[pl.ds(r0, sub), ka:kb]
            v = v_ref[pl.ds(r0, sub), va:vb]
            b = b_all[:, ka:kb]
            st = st_scr[hd]
            o = _dot_nt((q * jnp.exp(b)).astype(BF16), st.astype(BF16))
            for j in range(sub):
                w = q * k[j:j + 1, :] * jnp.exp(jnp.minimum(b - b[j:j + 1, :], 0.0))
                a = jnp.sum(w, axis=-1, keepdims=True)
                o = o + jnp.where(rowi >= j, a, 0.0) * v[j:j + 1, :]
            o_ref[pl.ds(r0, sub), va:vb] = o
            bl = b[sub - 1:sub, :]
            kt = (k * jnp.exp(bl - b)).astype(BF16)
            st_scr[hd] = st * jnp.exp(bl) + _dot_tn(v.astype(BF16), kt)
        return carry

    lax.fori_loop(0, GLA_TOKENS // sub, chunk, 0)

    @pl.when(c == pl.num_programs(1) - 1)
    def _():
        for hd in range(GLA_HEADS):
            sout_ref[0, hd] = st_scr[hd].T


def _gla_prompt(gq, gk, gv, gg, batch, seq):
    tc = GLA_TOKENS
    nc = seq // tc

    def rows(w):
        return pl.BlockSpec((tc, w), lambda b, c: (b * nc + c, 0))

    return pl.pallas_call(
        _gla_prompt_kernel,
        grid=(batch, nc),
        in_specs=[rows(GLA_QK_W), rows(GLA_QK_W), rows(GLA_V_W), rows(GLA_QK_W)],
        out_specs=[rows(GLA_V_W), pl.BlockSpec((1, GLA_HEADS, GLA_DK, GLA_DV), lambda b, c: (b, 0, 0, 0))],
        out_shape=[jax.ShapeDtypeStruct((batch * seq, GLA_V_W), F32),
                   jax.ShapeDtypeStruct((batch, GLA_HEADS, GLA_DK, GLA_DV), F32)],
        scratch_shapes=[pltpu.VMEM((GLA_HEADS, GLA_DV, GLA_DK), F32)],
        compiler_params=_params("parallel", "arbitrary"),
    )(gq, gk, gv, gg)


def _gla_sample_kernel(qt_ref, kt_ref, gt_ref, v_ref, s_ref, snew_ref, o_ref):
    c = pl.program_id(1)
    n_seq = s_ref.shape[0]

    @pl.when(c == 0)
    def _():
        o_ref[...] = jnp.zeros(o_ref.shape, F32)

    qt = qt_ref[...]
    kt = kt_ref[...]
    at = jnp.exp(gt_ref[...])
    for s in range(n_seq):
        sn = at[:, s:s + 1] * s_ref[s, 0] + kt[:, s:s + 1] * v_ref[s:s + 1, :]
        snew_ref[s, 0] = sn
        o_ref[s:s + 1, :] += jnp.sum(qt[:, s:s + 1] * sn, axis=0, keepdims=True)


def _gla_sample(qt, kt, gt, v, state):
    n_seq = state.shape[0]
    rb = GLA_S_DK_BLOCK
    nb = GLA_DK // rb
    vec = pl.BlockSpec((rb, n_seq), lambda h, c: (h * nb + c, 0))
    st = pl.BlockSpec((n_seq, 1, rb, GLA_DV), lambda h, c: (0, h, c, 0))
    return pl.pallas_call(
        _gla_sample_kernel,
        grid=(GLA_HEADS, nb),
        in_specs=[vec, vec, vec, pl.BlockSpec((n_seq, GLA_DV), lambda h, c: (0, h)), st],
        out_specs=[st, pl.BlockSpec((n_seq, GLA_DV), lambda h, c: (0, h))],
        out_shape=[jax.ShapeDtypeStruct(state.shape, F32), jax.ShapeDtypeStruct((n_seq, GLA_V_W), F32)],
        compiler_params=_params("parallel", "arbitrary"),
    )(qt, kt, gt, v, state)


def _out_kernel(x_ref, go_ref, gr_ref, mo_ref, gn_ref, wout_ref, o_ref):
    gn = gn_ref[...]
    gr = gr_ref[...]
    acc = x_ref[...] + _dot(mo_ref[...].astype(BF16), wout_ref[GLA_V_W:, :])
    parts = []
    for hd in range(GLA_HEADS):
        a, b = hd * GLA_DV, (hd + 1) * GLA_DV
        parts.append(_rms(go_ref[:, a:b], gn) * jax.nn.silu(gr[:, a:b]))
    go = jnp.concatenate(parts, axis=-1).astype(BF16)
    o_ref[...] = acc + _dot(go, wout_ref[:GLA_V_W, :])


def _out_proj(x_all, gla_o, gr, mla_o, gn, wout):
    n = x_all.shape[0]
    tb = OUT_TOKENS

    def rows(w):
        return pl.BlockSpec((tb, w), lambda i: (i, 0))

    return pl.pallas_call(
        _out_kernel,
        grid=(n // tb,),
        in_specs=[rows(D_MODEL), rows(GLA_V_W), rows(GLA_V_W), rows(MLA_V_W), _const_spec(gn.shape),
                  _const_spec(wout.shape)],
        out_specs=rows(D_MODEL),
        out_shape=jax.ShapeDtypeStruct((n, D_MODEL), F32),
        compiler_params=_params("parallel"),
    )(x_all, gla_o, gr, mla_o, gn, wout)


def _topk_rows_multi(xs, ids, k):
    xs = list(xs)
    vals = [[] for _ in xs]
    poss = [[] for _ in xs]
    for _ in range(k):
        for n, x in enumerate(xs):
            m = jnp.max(x, axis=0, keepdims=True)
            p = jnp.min(jnp.where(x == m, ids, jnp.inf), axis=0, keepdims=True)
            xs[n] = jnp.where(ids == p, -jnp.inf, x)
            vals[n].append(m)
            poss[n].append(p)
    return [(jnp.concatenate(v, axis=0), jnp.concatenate(p, axis=0)) for v, p in zip(vals, poss)]


def _topk_rows(x, ids, k):
    return _topk_rows_multi([x], ids, k)[0]


def _row_ids(n_rows, n_lanes, start, step):
    return (lax.broadcasted_iota(jnp.int32, (n_rows, n_lanes), 0) * step + start).astype(F32)


def _pair_candidates(s1, s2):
    k = PEER_TOPK
    n = s1.shape[1]
    lo = lax.broadcasted_iota(jnp.int32, (k, n), 0) < 4
    vals = [s1[0:1, :] + s2]
    ids = [_row_ids(k, n, 0, 1)]
    for r in (1, 2, 3):
        vals.append(s1[r:r + 1, :] + s2[0:8, :])
        ids.append(_row_ids(8, n, r * k, 1))
    vals.append(jnp.where(lo, -jnp.inf, s1 + s2[0:1, :]))
    ids.append(_row_ids(k, n, 0, k))
    for c in (1, 2):
        vals.append(jnp.where(lo[0:8], -jnp.inf, s1[0:8, :] + s2[c:c + 1, :]))
        ids.append(_row_ids(8, n, c, k))
    return jnp.concatenate(vals, axis=0), jnp.concatenate(ids, axis=0)


def _peerq_kernel(x_ref, gffn_ref, wq_ref, keys_ref, xn_ref, a_ref, b_ref, g_ref,
                  q_scr, v_scr, i_scr, sa_scr, sb_scr, sg_scr):
    topk = PEER_TOPK
    half = PEER_DKEY // 2
    xn = _rms(x_ref[...], gffn_ref[...]).astype(BF16)
    xn_ref[...] = xn
    q = _dot(xn, wq_ref[...])
    for hp in range(2 * PEER_HEADS):
        q_scr[hp] = q[:, hp * half:(hp + 1) * half].astype(BF16)

    key_ids = _row_ids(PEER_NKEYS, LANES, 0, 1)

    def side(it, carry):
        hps = [it * PEERQ_SIDES + u for u in range(PEERQ_SIDES)]
        scores = [_dot_nt(keys_ref[hp], q_scr[hp]) for hp in hps]
        for hp, (vals, poss) in zip(hps, _topk_rows_multi(scores, key_ids, topk)):
            v_scr[hp] = vals
            i_scr[hp] = poss
        return carry

    lax.fori_loop(0, 2 * PEER_HEADS // PEERQ_SIDES, side, 0)

    r16 = _row_ids(topk, LANES, 0, 1)

    def head(it, carry):
        hds = [it * PEERQ_HEADS + u for u in range(PEERQ_HEADS)]
        cands = [_pair_candidates(v_scr[2 * hd], v_scr[2 * hd + 1]) for hd in hds]
        tops = _topk_rows_multi([cv for cv, _ in cands], cands[0][1], topk)
        for hd, (top_s, pos) in zip(hds, tops):
            i1, i2 = i_scr[2 * hd], i_scr[2 * hd + 1]
            e = jnp.exp(top_s - top_s[0:1, :])
            gate = e / jnp.sum(e, axis=0, keepdims=True)
            r = jnp.floor(pos * (1.0 / topk))
            cc = pos - r * topk
            a_rows, b_rows = [], []
            for kk in range(topk):
                a_rows.append(jnp.sum(jnp.where(r16 == r[kk:kk + 1, :], i1, 0.0), axis=0, keepdims=True))
                b_rows.append(jnp.sum(jnp.where(r16 == cc[kk:kk + 1, :], i2, 0.0), axis=0, keepdims=True))
            base = pl.multiple_of(hd * topk, topk)
            sa_scr[pl.ds(base, topk), :] = jnp.concatenate(a_rows, axis=0)
            sb_scr[pl.ds(base, topk), :] = jnp.concatenate(b_rows, axis=0)
            sg_scr[pl.ds(base, topk), :] = gate
        return carry

    lax.fori_loop(0, PEER_HEADS // PEERQ_HEADS, head, 0)
    a_ref[...] = sa_scr[...].T
    b_ref[...] = sb_scr[...].T
    g_ref[...] = sg_scr[...].T


def _peer_query(x2, gffn, wq, keys2):
    n = x2.shape[0]
    tb = PEERQ_TOKENS
    nslot = PEER_HEADS * PEER_TOPK
    half = PEER_DKEY // 2

    def rows(w):
        return pl.BlockSpec((tb, w), lambda i: (i, 0))

    return pl.pallas_call(
        _peerq_kernel,
        grid=(n // tb,),
        in_specs=[rows(D_MODEL), _const_spec(gffn.shape), _const_spec(wq.shape), _const_spec(keys2.shape)],
        out_specs=[rows(D_MODEL), rows(nslot), rows(nslot), rows(nslot)],
        out_shape=[jax.ShapeDtypeStruct((n, D_MODEL), BF16)] + [jax.ShapeDtypeStruct((n, nslot), F32)] * 3,
        scratch_shapes=[pltpu.VMEM((2 * PEER_HEADS, tb, half), BF16),
                        pltpu.VMEM((2 * PEER_HEADS, PEER_TOPK, tb), F32),
                        pltpu.VMEM((2 * PEER_HEADS, PEER_TOPK, tb), F32),
                        pltpu.VMEM((nslot, tb), F32), pltpu.VMEM((nslot, tb), F32), pltpu.VMEM((nslot, tb), F32)],
        compiler_params=_params("parallel"),
    )(x2, gffn, wq, keys2)


def _peer_dense_kernel(xn_ref, a_ref, b_ref, g_ref, ut_ref, v_ref, x2_ref, gfin_ref, out_ref, gp_scr):
    tb = PEER_TOKENS
    te = PEER_EXPERTS
    stride = tb + PEER_G_PAD
    j = pl.program_id(1)

    @pl.when(j == 0)
    def _():
        out_ref[...] = jnp.zeros(out_ref.shape, F32)
        sub_iota = lax.broadcasted_iota(jnp.int32, (PEER_NKEYS, LANES), 0).astype(F32)

        def tok(t8, carry):
            for u in range(PEER_G_UNROLL):
                t = t8 * PEER_G_UNROLL + u
                arow = a_ref[pl.ds(t, 1), :]
                brow = b_ref[pl.ds(t, 1), :]
                grow = g_ref[pl.ds(t, 1), :]
                at = jnp.where(sub_iota == arow, 1.0, 0.0).astype(BF16)
                bt = jnp.where(sub_iota == brow, grow, 0.0).astype(BF16)
                gp_scr[pl.ds(t, PEER_NKEYS, stride=stride), :] = _dot_nt(at, bt)
            return carry

        lax.fori_loop(0, tb // PEER_G_UNROLL, tok, 0)

    act = _gelu(_dot(xn_ref[...], ut_ref[0]))
    n_a = te // LANES
    parts = []
    for jj in range(n_a):
        r0 = pl.multiple_of((j * n_a + jj) * stride, 8)
        parts.append(gp_scr[pl.ds(r0, tb), :] * act[:, jj * LANES:(jj + 1) * LANES])
    p = jnp.concatenate(parts, axis=-1).astype(BF16)
    out_ref[...] += _dot(p, v_ref[...])

    @pl.when(j == pl.num_programs(1) - 1)
    def _():
        out_ref[...] = _rms(x2_ref[...] + out_ref[...], gfin_ref[...])


def _peer_dense(xn, sa, sb, sg, ut, vv, x2, gfin):
    n = xn.shape[0]
    tb = PEER_TOKENS
    te = PEER_EXPERTS
    n_exp = vv.shape[0]
    nslot = PEER_HEADS * PEER_TOPK

    def rows(w):
        return pl.BlockSpec((tb, w), lambda i, j: (i, 0))

    return pl.pallas_call(
        _peer_dense_kernel,
        grid=(n // tb, n_exp // te),
        in_specs=[rows(D_MODEL), rows(nslot), rows(nslot), rows(nslot),
                  pl.BlockSpec((1, D_MODEL, te), lambda i, j: (j, 0, 0)),
                  pl.BlockSpec((te, D_MODEL), lambda i, j: (j, 0)),
                  rows(D_MODEL), _const_spec(gfin.shape)],
        out_specs=rows(D_MODEL),
        out_shape=jax.ShapeDtypeStruct((n, D_MODEL), F32),
        scratch_shapes=[pltpu.VMEM((PEER_NKEYS * (tb + PEER_G_PAD), LANES), F32)],
        compiler_params=_params("parallel", "arbitrary"),
    )(xn, sa, sb, sg, ut, vv, x2, gfin)


def _rope_tables(pos):
    half = MLA_ROPE // 2
    inv = ROPE_THETA ** (-jnp.arange(half, dtype=F32) / half)
    ang = pos.astype(F32)[:, None] * inv[None, :]
    cos, sin = jnp.cos(ang), jnp.sin(ang)
    n = pos.shape[0]
    cc = jnp.concatenate([cos, cos], axis=-1)
    ss = jnp.concatenate([-sin, sin], axis=-1)
    z64 = jnp.zeros((n, LANES - MLA_ROPE), F32)
    cq_tab = jnp.concatenate([jnp.ones((n, MLA_NOPE), F32), cc, z64], axis=-1)
    sq_tab = jnp.concatenate([jnp.zeros((n, MLA_NOPE), F32), ss, z64], axis=-1)
    ck_tab = jnp.concatenate([cc, z64], axis=-1)
    sk_tab = jnp.concatenate([ss, z64], axis=-1)
    return cq_tab, sq_tab, ck_tab, sk_tab


def _prep_win(w_in):
    d = w_in.shape[0]
    o = 0
    pieces = {}
    for name, width in (("gq", GLA_QK_W), ("gk", GLA_QK_W), ("gv", GLA_V_W), ("gr", GLA_V_W), ("ga", GLA_LOWRANK),
                        ("cq", MLA_Q_RANK), ("ckv", MLA_KV_RANK), ("kpe", MLA_ROPE)):
        pieces[name] = w_in[:, o:o + width]
        o += width
    half = MLA_ROPE // 2
    kpe = pieces["kpe"]
    ksw = jnp.concatenate([kpe[:, half:], kpe[:, :half]], axis=-1)
    z = lambda w: jnp.zeros((d, w), w_in.dtype)
    cols = [pieces["gq"], pieces["gk"], pieces["gv"], pieces["gr"], pieces["cq"], pieces["ckv"],
            kpe, pieces["ga"], z(LANES - MLA_ROPE - GLA_LOWRANK), ksw, z(LANES - MLA_ROPE)]
    return jnp.concatenate(cols, axis=-1).astype(BF16)


def _prep_wuq(w_uq):
    r = w_uq.shape[0]
    w = w_uq.reshape(r, MLA_HEADS, MLA_NOPE + MLA_ROPE)
    half = MLA_ROPE // 2
    nope, pe = w[..., :MLA_NOPE], w[..., MLA_NOPE:]
    pad = jnp.zeros((r, MLA_HEADS, MLA_QHEAD_PAD - MLA_NOPE - MLA_ROPE), w_uq.dtype)
    plain = jnp.concatenate([nope, pe, pad], axis=-1)
    swapped = jnp.concatenate([jnp.zeros_like(nope), pe[..., half:], pe[..., :half], pad], axis=-1)
    width = MLA_HEADS * MLA_QHEAD_PAD
    return plain.reshape(r, width).astype(BF16), swapped.reshape(r, width).astype(BF16)


def kernel(x_prompt, x_sample, cache_ckv, cache_kpe, state_gla, page_table, norm_attn, w_in, gla_w_alpha, gla_b_alpha, gla_norm, mla_q_norm, mla_w_uq, mla_kv_norm, mla_w_uk, mla_w_uv, w_out, norm_ffn, peer_w_q, peer_keys, peer_u, peer_v, norm_final):
    batch, seq, d = x_prompt.shape
    n_seq, dec_seq, _ = x_sample.shape
    depth = w_in.shape[0]
    assert depth == 1 and dec_seq == 1 and d == D_MODEL
    n_prompt = batch * seq
    n = n_prompt + n_seq
    past_len = page_table.shape[1] * PAGE_SIZE

    x_all = jnp.concatenate([x_prompt.reshape(n_prompt, d), x_sample.reshape(n_seq, d)], axis=0)
    pos = jnp.concatenate([jnp.tile(jnp.arange(seq, dtype=jnp.int32), batch),
                           jnp.full((n_seq,), past_len, jnp.int32)])
    tabs = _rope_tables(pos)

    row = lambda g: g.reshape(1, -1).astype(F32)
    win_p = _prep_win(w_in[0])
    walpha_p = jnp.zeros((LANES, GLA_QK_W), F32).at[MLA_ROPE:MLA_ROPE + GLA_LOWRANK].set(gla_w_alpha[0]).astype(BF16)
    wuq_p, wuqs_p = _prep_wuq(mla_w_uq[0])

    gq, gk, gv, gr, gg, qm, ckv, kpe128 = _proj(
        x_all, tabs, row(norm_attn[0]), win_p, walpha_p, row(gla_b_alpha[0]), row(mla_q_norm[0]),
        wuq_p, wuqs_p, row(mla_kv_norm[0]))

    wuk2 = mla_w_uk[0].reshape(MLA_KV_RANK, MLA_HEADS * MLA_NOPE).astype(BF16)
    wuv2 = mla_w_uv[0].reshape(MLA_KV_RANK, MLA_V_W).astype(BF16)
    kfull, vfull = _kv_up(ckv, kpe128, wuk2, wuv2, n_prompt)
    mla_o_p = _mla_prompt(qm, kfull, vfull, batch, seq)

    wuk_h = jnp.transpose(mla_w_uk[0], (1, 0, 2)).astype(BF16)
    wuv_h = jnp.transpose(mla_w_uv[0], (1, 0, 2)).astype(BF16)
    qfull = jnp.transpose(_qlat(qm, wuk_h, n_prompt, n_seq), (1, 0, 2))
    olat = _decode(page_table, qfull, ckv[n_prompt:].reshape(n_seq, 1, MLA_KV_RANK),
                   kpe128[n_prompt:].reshape(n_seq, 1, LANES), cache_ckv, jnp.swapaxes(cache_kpe, 2, 3))
    mla_o_s = _o_up(jnp.transpose(olat, (1, 0, 2)), wuv_h)

    gla_o_p, gla_state_p = _gla_prompt(gq, gk, gv, gg, batch, seq)
    gla_state_s, gla_o_s = _gla_sample(gq[n_prompt:].T, gk[n_prompt:].T, gg[n_prompt:].T, gv[n_prompt:],
                                       state_gla[0])

    gla_o = jnp.concatenate([gla_o_p, gla_o_s], axis=0)
    mla_o = jnp.concatenate([mla_o_p, mla_o_s], axis=0)
    x2 = _out_proj(x_all, gla_o, gr, mla_o, row(gla_norm[0]), w_out[0].astype(BF16))

    keys2 = peer_keys[0].reshape(2 * PEER_HEADS, PEER_NKEYS, PEER_DKEY // 2).astype(BF16)
    xn, sa, sb, sg = _peer_query(x2, row(norm_ffn[0]), peer_w_q[0].astype(BF16), keys2)
    ut = jnp.transpose(peer_u[0].astype(BF16).reshape(-1, PEER_EXPERTS, d), (0, 2, 1))
    y = _peer_dense(xn, sa, sb, sg, ut, peer_v[0].astype(BF16), x2, row(norm_final))

    y_prompt = y[:n_prompt].reshape(batch, seq, d)
    y_sample = y[n_prompt:].reshape(n_seq, dec_seq, d)
    ckv_p = ckv[:n_prompt].reshape(1, batch, seq, MLA_KV_RANK)
    kpe_p = kpe128[:n_prompt, :MLA_ROPE].reshape(1, batch, seq, MLA_ROPE)
    ckv_s = ckv[n_prompt:].reshape(1, n_seq, dec_seq, MLA_KV_RANK)
    kpe_s = kpe128[n_prompt:, :MLA_ROPE].reshape(1, n_seq, dec_seq, MLA_ROPE)
    return (y_prompt, y_sample, ckv_p, kpe_p, gla_state_p[None], ckv_s, kpe_s, gla_state_s[None])
```

```python
import functools

import jax
import jax.numpy as jnp
from jax import lax
from jax.experimental import pallas as pl
from jax.experimental.pallas import tpu as pltpu

F32 = jnp.float32
BF16 = jnp.bfloat16

D_MODEL = 2048
PAGE_SIZE = 128
GLA_HEADS = 4
GLA_DK = 128
GLA_DV = 256
GLA_LOWRANK = 16
GLA_TAU = 16.0
MLA_HEADS = 8
MLA_Q_RANK = 512
MLA_KV_RANK = 512
MLA_NOPE = 128
MLA_ROPE = 64
MLA_DV = 128
MLA_SCALE = (MLA_NOPE + MLA_ROPE) ** -0.5
ROPE_THETA = 10000.0
PEER_HEADS = 8
PEER_NKEYS = 128
PEER_DKEY = 256
PEER_TOPK = 16
EPS = 1e-6

GLA_QK_W = GLA_HEADS * GLA_DK
GLA_V_W = GLA_HEADS * GLA_DV
MLA_V_W = MLA_HEADS * MLA_DV
MLA_QHEAD_PAD = 256
LANES = 128

VMEM_LIMIT_BYTES = 56 * 1024 * 1024

PROJ_TOKENS = 320
KV_TOKENS = 512
ATTN_BLOCK = 512
ATTN_HEADS = 4
GLA_TOKENS = 256
GLA_SUB = 16
DEC_PAGES = 32
GLA_S_DK_BLOCK = 32
OUT_TOKENS = 320
PEERQ_TOKENS = 128
PEERQ_SIDES = 4
PEERQ_HEADS = 2
PEER_TOKENS = 416
PEER_EXPERTS = 512
PEER_G_PAD = 8
PEER_G_UNROLL = 8


def _params(*sem):
    return pltpu.CompilerParams(dimension_semantics=sem, vmem_limit_bytes=VMEM_LIMIT_BYTES)


def _const_spec(shape):
    nd = len(shape)
    return pl.BlockSpec(shape, lambda *_: (0,) * nd, pipeline_mode=pl.Buffered(1))


def _rms(x, g):
    return x * lax.rsqrt(jnp.mean(x * x, axis=-1, keepdims=True) + EPS) * g


def _gelu(x):
    return 0.5 * x * (1.0 + lax.erf(x * (0.5 ** 0.5)))


def _dot(a, b):
    return jnp.dot(a, b, preferred_element_type=F32)


def _dot_nt(a, b):
    return lax.dot_general(a, b, (((1,), (1,)), ((), ())), preferred_element_type=F32)


def _dot_tn(a, b):
    return lax.dot_general(a, b, (((0,), (0,)), ((), ())), preferred_element_type=F32)


_C_GQ, _C_GK, _C_GV, _C_GR = 0, 512, 1024, 2048
_C_CQ, _C_CKV, _C_KPE, _C_KSW, _C_END = 3072, 3584, 4096, 4224, 4352


def _proj_kernel(x_ref, gattn_ref, win_ref, walpha_ref, balpha_ref, gqn_ref, wuq_ref, wuqs_ref, gkv_ref,
                 cq_ref, sq_ref, ck_ref, sk_ref,
                 oq_ref, ok_ref, ov_ref, or_ref, og_ref, oqm_ref, ockv_ref, okpe_ref):
    h = _rms(x_ref[...], gattn_ref[...]).astype(BF16)

    def col(a, b):
        return _dot(h, win_ref[:, a:b])

    oq_ref[...] = col(_C_GQ, _C_GK) * (GLA_DK ** -0.5)
    ok_ref[...] = col(_C_GK, _C_GV)
    ov_ref[...] = col(_C_GV, _C_GR)
    or_ref[...] = col(_C_GR, _C_CQ)
    grp = col(_C_KPE, _C_KSW)
    swp = col(_C_KSW, _C_END)
    xa = _dot(grp.astype(BF16), walpha_ref[...]) + balpha_ref[...]
    og_ref[...] = jax.nn.log_sigmoid(xa) / GLA_TAU
    okpe_ref[...] = grp * ck_ref[...] + swp * sk_ref[...]
    ockv_ref[...] = _rms(col(_C_CKV, _C_KPE), gkv_ref[...])
    cqn = _rms(col(_C_CQ, _C_CKV), gqn_ref[...]).astype(BF16)
    cq_tab = cq_ref[...]
    sq_tab = sq_ref[...]
    for hd in range(MLA_HEADS):
        a, b = hd * MLA_QHEAD_PAD, (hd + 1) * MLA_QHEAD_PAD
        raw = _dot(cqn, wuq_ref[:, a:b])
        sw = _dot(cqn, wuqs_ref[:, a:b])
        oqm_ref[:, a:b] = (raw * cq_tab + sw * sq_tab).astype(BF16)


def _proj(x_all, tabs, gattn, win_p, walpha_p, balpha, gqn, wuq_p, wuqs_p, gkv):
    n = x_all.shape[0]
    tb = PROJ_TOKENS
    cq_tab, sq_tab, ck_tab, sk_tab = tabs

    def rows(w):
        return pl.BlockSpec((tb, w), lambda i: (i, 0))

    out_w = (GLA_QK_W, GLA_QK_W, GLA_V_W, GLA_V_W, GLA_QK_W, MLA_HEADS * MLA_QHEAD_PAD, MLA_KV_RANK, LANES)
    out_dt = (F32, F32, F32, F32, F32, BF16, F32, F32)
    return pl.pallas_call(
        _proj_kernel,
        grid=(n // tb,),
        in_specs=[rows(D_MODEL), _const_spec(gattn.shape), _const_spec(win_p.shape), _const_spec(walpha_p.shape),
                  _const_spec(balpha.shape), _const_spec(gqn.shape), _const_spec(wuq_p.shape),
                  _const_spec(wuqs_p.shape), _const_spec(gkv.shape),
                  rows(MLA_QHEAD_PAD), rows(MLA_QHEAD_PAD), rows(LANES), rows(LANES)],
        out_specs=[rows(w) for w in out_w],
        out_shape=[jax.ShapeDtypeStruct((n, w), dt) for w, dt in zip(out_w, out_dt)],
        compiler_params=_params("parallel"),
    )(x_all, gattn, win_p, walpha_p, balpha, gqn, wuq_p, wuqs_p, gkv, cq_tab, sq_tab, ck_tab, sk_tab)


def _kv_kernel(ckv_ref, kpe_ref, wuk_ref, wuv_ref, k_ref, v_ref):
    c = ckv_ref[...].astype(BF16)
    kpe = kpe_ref[...].astype(BF16)
    kn = _dot(c, wuk_ref[...]).astype(BF16)
    v_ref[...] = _dot(c, wuv_ref[...]).astype(BF16)
    for hd in range(MLA_HEADS):
        a = hd * MLA_QHEAD_PAD
        k_ref[:, a:a + MLA_NOPE] = kn[:, hd * MLA_NOPE:(hd + 1) * MLA_NOPE]
        k_ref[:, a + MLA_NOPE:a + MLA_QHEAD_PAD] = kpe


def _kv_up(ckv, kpe128, wuk2, wuv2, n_prompt):
    tb = KV_TOKENS
    return pl.pallas_call(
        _kv_kernel,
        grid=(n_prompt // tb,),
        in_specs=[pl.BlockSpec((tb, MLA_KV_RANK), lambda i: (i, 0)), pl.BlockSpec((tb, LANES), lambda i: (i, 0)),
                  _const_spec(wuk2.shape), _const_spec(wuv2.shape)],
        out_specs=[pl.BlockSpec((tb, MLA_HEADS * MLA_QHEAD_PAD), lambda i: (i, 0)),
                   pl.BlockSpec((tb, MLA_V_W), lambda i: (i, 0))],
        out_shape=[jax.ShapeDtypeStruct((n_prompt, MLA_HEADS * MLA_QHEAD_PAD), BF16),
                   jax.ShapeDtypeStruct((n_prompt, MLA_V_W), BF16)],
        compiler_params=_params("parallel"),
    )(ckv, kpe128, wuk2, wuv2)


def _mla_prompt_kernel(q_ref, k_ref, v_ref, o_ref, m_scr, l_scr, acc_scr):
    blk = ATTN_BLOCK
    i = pl.program_id(2)
    m_scr[...] = jnp.full(m_scr.shape, -jnp.inf, F32)
    l_scr[...] = jnp.zeros(l_scr.shape, F32)
    acc_scr[...] = jnp.zeros(acc_scr.shape, F32)

    def step(kb, masked):
        r0 = pl.multiple_of(kb * blk, blk)

        def scores(g):
            qa, qb = g * MLA_QHEAD_PAD, (g + 1) * MLA_QHEAD_PAD
            s = _dot_nt(q_ref[:, qa:qb], k_ref[pl.ds(r0, blk), qa:qb]) * MLA_SCALE
            if masked:
                row = lax.broadcasted_iota(jnp.int32, (blk, blk), 0)
                colm = lax.broadcasted_iota(jnp.int32, (blk, blk), 1)
                s = jnp.where(row >= colm, s, -jnp.inf)
            return s

        def softmax_pv(g, s):
            m_old = m_scr[g]
            m_new = jnp.maximum(m_old, jnp.max(s, axis=-1, keepdims=True))
            alpha = jnp.exp(m_old - m_new)
            p = jnp.exp(s - m_new)
            l_scr[g] = alpha * l_scr[g] + jnp.sum(p, axis=-1, keepdims=True)
            acc_scr[g] = alpha * acc_scr[g] + _dot(p.astype(BF16), v_ref[pl.ds(r0, blk), g * MLA_DV:(g + 1) * MLA_DV])
            m_scr[g] = m_new

        s_next = scores(0)
        for g in range(ATTN_HEADS):
            s_cur = s_next
            if g + 1 < ATTN_HEADS:
                s_next = scores(g + 1)
            softmax_pv(g, s_cur)

    def body(kb, carry):
        step(kb, False)
        return carry

    lax.fori_loop(0, i, body, 0)
    step(i, True)
    for g in range(ATTN_HEADS):
        o_ref[:, g * MLA_DV:(g + 1) * MLA_DV] = acc_scr[g] / l_scr[g]


def _mla_prompt(qm, kfull, v, batch, seq):
    blk = ATTN_BLOCK
    nq = seq // blk
    hg = ATTN_HEADS
    return pl.pallas_call(
        _mla_prompt_kernel,
        grid=(batch, MLA_HEADS // hg, nq),
        in_specs=[pl.BlockSpec((blk, hg * MLA_QHEAD_PAD), lambda b, h, i: (b * nq + i, h)),
                  pl.BlockSpec((seq, hg * MLA_QHEAD_PAD), lambda b, h, i: (b, h)),
                  pl.BlockSpec((seq, hg * MLA_DV), lambda b, h, i: (b, h))],
        out_specs=pl.BlockSpec((blk, hg * MLA_DV), lambda b, h, i: (b * nq + i, h)),
        out_shape=jax.ShapeDtypeStruct((batch * seq, MLA_V_W), F32),
        scratch_shapes=[pltpu.VMEM((hg, blk, 1), F32), pltpu.VMEM((hg, blk, 1), F32),
                        pltpu.VMEM((hg, blk, MLA_DV), F32)],
        compiler_params=_params("parallel", "parallel", "arbitrary"),
    )(qm, kfull, v)


def _qlat_kernel(q_ref, wuk_ref, o_ref):
    q = q_ref[...]
    qlat = _dot_nt(q[:, :MLA_NOPE], wuk_ref[0])
    o_ref[0] = jnp.concatenate([qlat, q[:, MLA_NOPE:MLA_NOPE + MLA_ROPE].astype(F32)], axis=-1).astype(BF16)


def _qlat(qm, wuk_h, n_prompt, n_sample):
    rb = n_prompt // n_sample
    width = MLA_KV_RANK + MLA_ROPE
    return pl.pallas_call(
        _qlat_kernel,
        grid=(MLA_HEADS,),
        in_specs=[pl.BlockSpec((n_sample, MLA_QHEAD_PAD), lambda h: (rb, h)),
                  pl.BlockSpec((1, MLA_KV_RANK, MLA_NOPE), lambda h: (h, 0, 0))],
        out_specs=pl.BlockSpec((1, n_sample, width), lambda h: (h, 0, 0)),
        out_shape=jax.ShapeDtypeStruct((MLA_HEADS, n_sample, width), BF16),
        compiler_params=_params("parallel"),
    )(qm, wuk_h)


def _decode_kernel(pt_ref, q_ref, cnew_ref, knew_ref, ckv_hbm, kpe_hbm, o_ref,
                   cbuf, kbuf, sem, m_scr, l_scr, acc_scr):
    np_ = DEC_PAGES
    s_id = pl.program_id(0)
    c = pl.program_id(1)
    n_chunks = pl.num_programs(1)
    n_steps = pl.num_programs(0) * n_chunks
    step = s_id * n_chunks + c
    slot = step % 2
    nxt = jnp.minimum(step + 1, n_steps - 1)
    nxt_seq = nxt // n_chunks
    nxt_page0 = (nxt % n_chunks) * np_
    rank = MLA_KV_RANK

    def page_copies(seq, page0, pg, slot_):
        page = pt_ref[seq, page0 + pg]
        return (pltpu.make_async_copy(ckv_hbm.at[0, page], cbuf.at[slot_, pg], sem.at[0, slot_]),
                pltpu.make_async_copy(kpe_hbm.at[0, page], kbuf.at[slot_, pg], sem.at[1, slot_]))

    def wait_slot(slot_):
        pltpu.make_async_copy(ckv_hbm.at[0, pl.ds(0, np_)], cbuf.at[slot_], sem.at[0, slot_]).wait()
        pltpu.make_async_copy(kpe_hbm.at[0, pl.ds(0, np_)], kbuf.at[slot_], sem.at[1, slot_]).wait()

    @pl.when(step == 0)
    def _():
        for pg in range(np_):
            for cp in page_copies(0, 0, pg, 0):
                cp.start()

    @pl.when(c == 0)
    def _():
        m_scr[...] = jnp.full(m_scr.shape, -jnp.inf, F32)
        l_scr[...] = jnp.zeros(l_scr.shape, F32)
        acc_scr[...] = jnp.zeros(acc_scr.shape, F32)

    wait_slot(slot)
    q = q_ref[0]
    q_lat, q_pe = q[:, :rank], q[:, rank:]
    s_parts = []
    for pg in range(np_):
        for cp in page_copies(nxt_seq, nxt_page0, pg, 1 - slot):
            cp.start()
        s_parts.append(_dot_nt(q_lat, cbuf[slot, pg].astype(BF16)) + _dot(q_pe, kbuf[slot, pg].astype(BF16)))
    s = jnp.concatenate(s_parts, axis=-1) * MLA_SCALE
    m_old = m_scr[...]
    m_new = jnp.maximum(m_old, jnp.max(s, axis=-1, keepdims=True))
    alpha = jnp.exp(m_old - m_new)
    p = jnp.exp(s - m_new)
    l_scr[...] = alpha * l_scr[...] + jnp.sum(p, axis=-1, keepdims=True)
    pb = p.astype(BF16)
    pv = _dot(pb[:, :PAGE_SIZE], cbuf[slot, 0].astype(BF16))
    for pg in range(1, np_):
        pv = pv + _dot(pb[:, pg * PAGE_SIZE:(pg + 1) * PAGE_SIZE], cbuf[slot, pg].astype(BF16))
    acc_scr[...] = alpha * acc_scr[...] + pv
    m_scr[...] = m_new

    @pl.when(step == n_steps - 1)
    def _():
        wait_slot(1 - slot)

    @pl.when(c == pl.num_programs(1) - 1)
    def _():
        cn = cnew_ref[0].astype(BF16).astype(F32)
        kn = knew_ref[0][:, :MLA_ROPE].astype(BF16).astype(F32)
        qf = q.astype(F32)
        s_new = (jnp.sum(qf[:, :rank] * cn, axis=-1, keepdims=True)
                 + jnp.sum(qf[:, rank:] * kn, axis=-1, keepdims=True)) * MLA_SCALE
        m_old2 = m_scr[...]
        m_fin = jnp.maximum(m_old2, s_new)
        a2 = jnp.exp(m_old2 - m_fin)
        p_new = jnp.exp(s_new - m_fin)
        l_fin = a2 * l_scr[...] + p_new
        acc = a2 * acc_scr[...] + p_new.astype(BF16).astype(F32) * cn
        o_ref[0] = acc / l_fin


def _decode(page_table, qfull, ckv_new, kpe_new, cache_ckv, cache_kpe_t):
    n_seq, n_pages = page_table.shape
    np_ = DEC_PAGES
    width = MLA_KV_RANK + MLA_ROPE

    grid_spec = pltpu.PrefetchScalarGridSpec(
        num_scalar_prefetch=1,
        grid=(n_seq, n_pages // np_),
        in_specs=[pl.BlockSpec((1, MLA_HEADS, width), lambda s, c, pt: (s, 0, 0)),
                  pl.BlockSpec((1, 1, MLA_KV_RANK), lambda s, c, pt: (s, 0, 0)),
                  pl.BlockSpec((1, 1, LANES), lambda s, c, pt: (s, 0, 0)),
                  pl.BlockSpec(memory_space=pl.ANY), pl.BlockSpec(memory_space=pl.ANY)],
        out_specs=pl.BlockSpec((1, MLA_HEADS, MLA_KV_RANK), lambda s, c, pt: (s, 0, 0)),
        scratch_shapes=[pltpu.VMEM((2, np_, PAGE_SIZE, MLA_KV_RANK), F32),
                        pltpu.VMEM((2, np_, MLA_ROPE, PAGE_SIZE), F32),
                        pltpu.SemaphoreType.DMA((2, 2)),
                        pltpu.VMEM((MLA_HEADS, 1), F32), pltpu.VMEM((MLA_HEADS, 1), F32),
                        pltpu.VMEM((MLA_HEADS, MLA_KV_RANK), F32)],
    )
    return pl.pallas_call(
        _decode_kernel,
        grid_spec=grid_spec,
        out_shape=jax.ShapeDtypeStruct((n_seq, MLA_HEADS, MLA_KV_RANK), F32),
        compiler_params=_params("arbitrary", "arbitrary"),
    )(page_table, qfull, ckv_new, kpe_new, cache_ckv, cache_kpe_t)


def _oup_kernel(o_ref, wuv_ref, out_ref):
    out_ref[...] = _dot(o_ref[0].astype(BF16), wuv_ref[0])


def _o_up(olat_h, wuv_h):
    n_sample = olat_h.shape[1]
    return pl.pallas_call(
        _oup_kernel,
        grid=(MLA_HEADS,),
        in_specs=[pl.BlockSpec((1, n_sample, MLA_KV_RANK), lambda h: (h, 0, 0)),
                  pl.BlockSpec((1, MLA_KV_RANK, MLA_DV), lambda h: (h, 0, 0))],
        out_specs=pl.BlockSpec((n_sample, MLA_DV), lambda h: (0, h)),
        out_shape=jax.ShapeDtypeStruct((n_sample, MLA_V_W), F32),
        compiler_params=_params("parallel"),
    )(olat_h, wuv_h)


def _gla_prompt_kernel(q_ref, k_ref, v_ref, g_ref, o_ref, sout_ref, st_scr):
    sub = GLA_SUB
    c = pl.program_id(1)

    @pl.when(c == 0)
    def _():
        st_scr[...] = jnp.zeros(st_scr.shape, F32)

    row = lax.broadcasted_iota(jnp.int32, (sub, sub), 0)
    colm = lax.broadcasted_iota(jnp.int32, (sub, sub), 1)
    tri = (row >= colm).astype(F32)
    rowi = lax.broadcasted_iota(jnp.int32, (sub, 1), 0)

    def chunk(ci, carry):
        r0 = pl.multiple_of(ci * sub, sub)
        b_all = jnp.dot(tri, g_ref[pl.ds(r0, sub), :], precision=lax.Precision.HIGHEST,
                        preferred_element_type=F32)
        for hd in range(GLA_HEADS):
            ka, kb = hd * GLA_DK, (hd + 1) * GLA_DK
            va, vb = hd * GLA_DV, (hd + 1) * GLA_DV
            q = q_ref[pl.ds(r0, sub), ka:kb]
            k = k_ref[pl.ds(r0, sub), ka:kb]
            v = v_ref[pl.ds(r0, sub), va:vb]
            b = b_all[:, ka:kb]
            st = st_scr[hd]
            o = _dot_nt((q * jnp.exp(b)).astype(BF16), st.astype(BF16))
            for j in range(sub):
                w = q * k[j:j + 1, :] * jnp.exp(jnp.minimum(b - b[j:j + 1, :], 0.0))
                a = jnp.sum(w, axis=-1, keepdims=True)
                o = o + jnp.where(rowi >= j, a, 0.0) * v[j:j + 1, :]
            o_ref[pl.ds(r0, sub), va:vb] = o
            bl = b[sub - 1:sub, :]
            kt = (k * jnp.exp(bl - b)).astype(BF16)
            st_scr[hd] = st * jnp.exp(bl) + _dot_tn(v.astype(BF16), kt)
        return carry

    lax.fori_loop(0, GLA_TOKENS // sub, chunk, 0)

    @pl.when(c == pl.num_programs(1) - 1)
    def _():
        for hd in range(GLA_HEADS):
            sout_ref[0, hd] = st_scr[hd].T


def _gla_prompt(gq, gk, gv, gg, batch, seq):
    tc = GLA_TOKENS
    nc = seq // tc

    def rows(w):
        return pl.BlockSpec((tc, w), lambda b, c: (b * nc + c, 0))

    return pl.pallas_call(
        _gla_prompt_kernel,
        grid=(batch, nc),
        in_specs=[rows(GLA_QK_W), rows(GLA_QK_W), rows(GLA_V_W), rows(GLA_QK_W)],
        out_specs=[rows(GLA_V_W), pl.BlockSpec((1, GLA_HEADS, GLA_DK, GLA_DV), lambda b, c: (b, 0, 0, 0))],
        out_shape=[jax.ShapeDtypeStruct((batch * seq, GLA_V_W), F32),
                   jax.ShapeDtypeStruct((batch, GLA_HEADS, GLA_DK, GLA_DV), F32)],
        scratch_shapes=[pltpu.VMEM((GLA_HEADS, GLA_DV, GLA_DK), F32)],
        compiler_params=_params("parallel", "arbitrary"),
    )(gq, gk, gv, gg)


def _gla_sample_kernel(qt_ref, kt_ref, gt_ref, v_ref, s_ref, snew_ref, o_ref):
    c = pl.program_id(1)
    n_seq = s_ref.shape[0]

    @pl.when(c == 0)
    def _():
        o_ref[...] = jnp.zeros(o_ref.shape, F32)

    qt = qt_ref[...]
    kt = kt_ref[...]
    at = jnp.exp(gt_ref[...])
    for s in range(n_seq):
        sn = at[:, s:s + 1] * s_ref[s, 0] + kt[:, s:s + 1] * v_ref[s:s + 1, :]
        snew_ref[s, 0] = sn
        o_ref[s:s + 1, :] += jnp.sum(qt[:, s:s + 1] * sn, axis=0, keepdims=True)


def _gla_sample(qt, kt, gt, v, state):
    n_seq = state.shape[0]
    rb = GLA_S_DK_BLOCK
    nb = GLA_DK // rb
    vec = pl.BlockSpec((rb, n_seq), lambda h, c: (h * nb + c, 0))
    st = pl.BlockSpec((n_seq, 1, rb, GLA_DV), lambda h, c: (0, h, c, 0))
    return pl.pallas_call(
        _gla_sample_kernel,
        grid=(GLA_HEADS, nb),
        in_specs=[vec, vec, vec, pl.BlockSpec((n_seq, GLA_DV), lambda h, c: (0, h)), st],
        out_specs=[st, pl.BlockSpec((n_seq, GLA_DV), lambda h, c: (0, h))],
        out_shape=[jax.ShapeDtypeStruct(state.shape, F32), jax.ShapeDtypeStruct((n_seq, GLA_V_W), F32)],
        compiler_params=_params("parallel", "arbitrary"),
    )(qt, kt, gt, v, state)


def _out_kernel(x_ref, go_ref, gr_ref, mo_ref, gn_ref, wout_ref, o_ref):
    gn = gn_ref[...]
    gr = gr_ref[...]
    acc = x_ref[...] + _dot(mo_ref[...].astype(BF16), wout_ref[GLA_V_W:, :])
    parts = []
    for hd in range(GLA_HEADS):
        a, b = hd * GLA_DV, (hd + 1) * GLA_DV
        parts.append(_rms(go_ref[:, a:b], gn) * jax.nn.silu(gr[:, a:b]))
    go = jnp.concatenate(parts, axis=-1).astype(BF16)
    o_ref[...] = acc + _dot(go, wout_ref[:GLA_V_W, :])


def _out_proj(x_all, gla_o, gr, mla_o, gn, wout):
    n = x_all.shape[0]
    tb = OUT_TOKENS

    def rows(w):
        return pl.BlockSpec((tb, w), lambda i: (i, 0))

    return pl.pallas_call(
        _out_kernel,
        grid=(n // tb,),
        in_specs=[rows(D_MODEL), rows(GLA_V_W), rows(GLA_V_W), rows(MLA_V_W), _const_spec(gn.shape),
                  _const_spec(wout.shape)],
        out_specs=rows(D_MODEL),
        out_shape=jax.ShapeDtypeStruct((n, D_MODEL), F32),
        compiler_params=_params("parallel"),
    )(x_all, gla_o, gr, mla_o, gn, wout)


def _topk_rows_multi(xs, ids, k):
    xs = list(xs)
    vals = [[] for _ in xs]
    poss = [[] for _ in xs]
    for _ in range(k):
        for n, x in enumerate(xs):
            m = jnp.max(x, axis=0, keepdims=True)
            p = jnp.min(jnp.where(x == m, ids, jnp.inf), axis=0, keepdims=True)
            xs[n] = jnp.where(ids == p, -jnp.inf, x)
            vals[n].append(m)
            poss[n].append(p)
    return [(jnp.concatenate(v, axis=0), jnp.concatenate(p, axis=0)) for v, p in zip(vals, poss)]


def _topk_rows(x, ids, k):
    return _topk_rows_multi([x], ids, k)[0]


def _row_ids(n_rows, n_lanes, start, step):
    return (lax.broadcasted_iota(jnp.int32, (n_rows, n_lanes), 0) * step + start).astype(F32)


def _pair_candidates(s1, s2):
    k = PEER_TOPK
    n = s1.shape[1]
    lo = lax.broadcasted_iota(jnp.int32, (k, n), 0) < 4
    vals = [s1[0:1, :] + s2]
    ids = [_row_ids(k, n, 0, 1)]
    for r in (1, 2, 3):
        vals.append(s1[r:r + 1, :] + s2[0:8, :])
        ids.append(_row_ids(8, n, r * k, 1))
    vals.append(jnp.where(lo, -jnp.inf, s1 + s2[0:1, :]))
    ids.append(_row_ids(k, n, 0, k))
    for c in (1, 2):
        vals.append(jnp.where(lo[0:8], -jnp.inf, s1[0:8, :] + s2[c:c + 1, :]))
        ids.append(_row_ids(8, n, c, k))
    return jnp.concatenate(vals, axis=0), jnp.concatenate(ids, axis=0)


def _peerq_kernel(x_ref, gffn_ref, wq_ref, keys_ref, xn_ref, a_ref, b_ref, g_ref,
                  q_scr, v_scr, i_scr, sa_scr, sb_scr, sg_scr):
    topk = PEER_TOPK
    half = PEER_DKEY // 2
    xn = _rms(x_ref[...], gffn_ref[...]).astype(BF16)
    xn_ref[...] = xn
    q = _dot(xn, wq_ref[...])
    for hp in range(2 * PEER_HEADS):
        q_scr[hp] = q[:, hp * half:(hp + 1) * half].astype(BF16)

    key_ids = _row_ids(PEER_NKEYS, LANES, 0, 1)

    def side(it, carry):
        hps = [it * PEERQ_SIDES + u for u in range(PEERQ_SIDES)]
        scores = [_dot_nt(keys_ref[hp], q_scr[hp]) for hp in hps]
        for hp, (vals, poss) in zip(hps, _topk_rows_multi(scores, key_ids, topk)):
            v_scr[hp] = vals
            i_scr[hp] = poss
        return carry

    lax.fori_loop(0, 2 * PEER_HEADS // PEERQ_SIDES, side, 0)

    r16 = _row_ids(topk, LANES, 0, 1)

    def head(it, carry):
        hds = [it * PEERQ_HEADS + u for u in range(PEERQ_HEADS)]
        cands = [_pair_candidates(v_scr[2 * hd], v_scr[2 * hd + 1]) for hd in hds]
        tops = _topk_rows_multi([cv for cv, _ in cands], cands[0][1], topk)
        for hd, (top_s, pos) in zip(hds, tops):
            i1, i2 = i_scr[2 * hd], i_scr[2 * hd + 1]
            e = jnp.exp(top_s - top_s[0:1, :])
            gate = e / jnp.sum(e, axis=0, keepdims=True)
            r = jnp.floor(pos * (1.0 / topk))
            cc = pos - r * topk
            a_rows, b_rows = [], []
            for kk in range(topk):
                a_rows.append(jnp.sum(jnp.where(r16 == r[kk:kk + 1, :], i1, 0.0), axis=0, keepdims=True))
                b_rows.append(jnp.sum(jnp.where(r16 == cc[kk:kk + 1, :], i2, 0.0), axis=0, keepdims=True))
            base = pl.multiple_of(hd * topk, topk)
            sa_scr[pl.ds(base, topk), :] = jnp.concatenate(a_rows, axis=0)
            sb_scr[pl.ds(base, topk), :] = jnp.concatenate(b_rows, axis=0)
            sg_scr[pl.ds(base, topk), :] = gate
        return carry

    lax.fori_loop(0, PEER_HEADS // PEERQ_HEADS, head, 0)
    a_ref[...] = sa_scr[...].T
    b_ref[...] = sb_scr[...].T
    g_ref[...] = sg_scr[...].T


def _peer_query(x2, gffn, wq, keys2):
    n = x2.shape[0]
    tb = PEERQ_TOKENS
    nslot = PEER_HEADS * PEER_TOPK
    half = PEER_DKEY // 2

    def rows(w):
        return pl.BlockSpec((tb, w), lambda i: (i, 0))

    return pl.pallas_call(
        _peerq_kernel,
        grid=(n // tb,),
        in_specs=[rows(D_MODEL), _const_spec(gffn.shape), _const_spec(wq.shape), _const_spec(keys2.shape)],
        out_specs=[rows(D_MODEL), rows(nslot), rows(nslot), rows(nslot)],
        out_shape=[jax.ShapeDtypeStruct((n, D_MODEL), BF16)] + [jax.ShapeDtypeStruct((n, nslot), F32)] * 3,
        scratch_shapes=[pltpu.VMEM((2 * PEER_HEADS, tb, half), BF16),
                        pltpu.VMEM((2 * PEER_HEADS, PEER_TOPK, tb), F32),
                        pltpu.VMEM((2 * PEER_HEADS, PEER_TOPK, tb), F32),
                        pltpu.VMEM((nslot, tb), F32), pltpu.VMEM((nslot, tb), F32), pltpu.VMEM((nslot, tb), F32)],
        compiler_params=_params("parallel"),
    )(x2, gffn, wq, keys2)


def _peer_dense_kernel(xn_ref, a_ref, b_ref, g_ref, ut_ref, v_ref, x2_ref, gfin_ref, out_ref, gp_scr):
    tb = PEER_TOKENS
    te = PEER_EXPERTS
    stride = tb + PEER_G_PAD
    j = pl.program_id(1)

    @pl.when(j == 0)
    def _():
        out_ref[...] = jnp.zeros(out_ref.shape, F32)
        sub_iota = lax.broadcasted_iota(jnp.int32, (PEER_NKEYS, LANES), 0).astype(F32)

        def tok(t8, carry):
            for u in range(PEER_G_UNROLL):
                t = t8 * PEER_G_UNROLL + u
                arow = a_ref[pl.ds(t, 1), :]
                brow = b_ref[pl.ds(t, 1), :]
                grow = g_ref[pl.ds(t, 1), :]
                at = jnp.where(sub_iota == arow, 1.0, 0.0).astype(BF16)
                bt = jnp.where(sub_iota == brow, grow, 0.0).astype(BF16)
                gp_scr[pl.ds(t, PEER_NKEYS, stride=stride), :] = _dot_nt(at, bt)
            return carry

        lax.fori_loop(0, tb // PEER_G_UNROLL, tok, 0)

    act = _gelu(_dot(xn_ref[...], ut_ref[0]))
    n_a = te // LANES
    parts = []
    for jj in range(n_a):
        r0 = pl.multiple_of((j * n_a + jj) * stride, 8)
        parts.append(gp_scr[pl.ds(r0, tb), :] * act[:, jj * LANES:(jj + 1) * LANES])
    p = jnp.concatenate(parts, axis=-1).astype(BF16)
    out_ref[...] += _dot(p, v_ref[...])

    @pl.when(j == pl.num_programs(1) - 1)
    def _():
        out_ref[...] = _rms(x2_ref[...] + out_ref[...], gfin_ref[...])


def _peer_dense(xn, sa, sb, sg, ut, vv, x2, gfin):
    n = xn.shape[0]
    tb = PEER_TOKENS
    te = PEER_EXPERTS
    n_exp = vv.shape[0]
    nslot = PEER_HEADS * PEER_TOPK

    def rows(w, **kw):
        return pl.BlockSpec((tb, w), lambda i, j: (i, 0), **kw)

    once = dict(pipeline_mode=pl.Buffered(1))
    return pl.pallas_call(
        _peer_dense_kernel,
        grid=(n // tb, n_exp // te),
        in_specs=[rows(D_MODEL, **once), rows(nslot, **once), rows(nslot, **once), rows(nslot, **once),
                  pl.BlockSpec((1, D_MODEL, te), lambda i, j: (j, 0, 0)),
                  pl.BlockSpec((te, D_MODEL), lambda i, j: (j, 0)),
                  rows(D_MODEL, **once), _const_spec(gfin.shape)],
        out_specs=rows(D_MODEL),
        out_shape=jax.ShapeDtypeStruct((n, D_MODEL), F32),
        scratch_shapes=[pltpu.VMEM((PEER_NKEYS * (tb + PEER_G_PAD), LANES), F32)],
        compiler_params=_params("parallel", "arbitrary"),
    )(xn, sa, sb, sg, ut, vv, x2, gfin)


def _rope_tables(pos):
    half = MLA_ROPE // 2
    inv = ROPE_THETA ** (-jnp.arange(half, dtype=F32) / half)
    ang = pos.astype(F32)[:, None] * inv[None, :]
    cos, sin = jnp.cos(ang), jnp.sin(ang)
    n = pos.shape[0]
    cc = jnp.concatenate([cos, cos], axis=-1)
    ss = jnp.concatenate([-sin, sin], axis=-1)
    z64 = jnp.zeros((n, LANES - MLA_ROPE), F32)
    cq_tab = jnp.concatenate([jnp.ones((n, MLA_NOPE), F32), cc, z64], axis=-1)
    sq_tab = jnp.concatenate([jnp.zeros((n, MLA_NOPE), F32), ss, z64], axis=-1)
    ck_tab = jnp.concatenate([cc, z64], axis=-1)
    sk_tab = jnp.concatenate([ss, z64], axis=-1)
    return cq_tab, sq_tab, ck_tab, sk_tab


def _prep_win(w_in):
    d = w_in.shape[0]
    o = 0
    pieces = {}
    for name, width in (("gq", GLA_QK_W), ("gk", GLA_QK_W), ("gv", GLA_V_W), ("gr", GLA_V_W), ("ga", GLA_LOWRANK),
                        ("cq", MLA_Q_RANK), ("ckv", MLA_KV_RANK), ("kpe", MLA_ROPE)):
        pieces[name] = w_in[:, o:o + width]
        o += width
    half = MLA_ROPE // 2
    kpe = pieces["kpe"]
    ksw = jnp.concatenate([kpe[:, half:], kpe[:, :half]], axis=-1)
    z = lambda w: jnp.zeros((d, w), w_in.dtype)
    cols = [pieces["gq"], pieces["gk"], pieces["gv"], pieces["gr"], pieces["cq"], pieces["ckv"],
            kpe, pieces["ga"], z(LANES - MLA_ROPE - GLA_LOWRANK), ksw, z(LANES - MLA_ROPE)]
    return jnp.concatenate(cols, axis=-1).astype(BF16)


def _prep_wuq(w_uq):
    r = w_uq.shape[0]
    w = w_uq.reshape(r, MLA_HEADS, MLA_NOPE + MLA_ROPE)
    half = MLA_ROPE // 2
    nope, pe = w[..., :MLA_NOPE], w[..., MLA_NOPE:]
    pad = jnp.zeros((r, MLA_HEADS, MLA_QHEAD_PAD - MLA_NOPE - MLA_ROPE), w_uq.dtype)
    plain = jnp.concatenate([nope, pe, pad], axis=-1)
    swapped = jnp.concatenate([jnp.zeros_like(nope), pe[..., half:], pe[..., :half], pad], axis=-1)
    width = MLA_HEADS * MLA_QHEAD_PAD
    return plain.reshape(r, width).astype(BF16), swapped.reshape(r, width).astype(BF16)


def kernel(x_prompt, x_sample, cache_ckv, cache_kpe, state_gla, page_table, norm_attn, w_in, gla_w_alpha, gla_b_alpha, gla_norm, mla_q_norm, mla_w_uq, mla_kv_norm, mla_w_uk, mla_w_uv, w_out, norm_ffn, peer_w_q, peer_keys, peer_u, peer_v, norm_final):
    batch, seq, d = x_prompt.shape
    n_seq, dec_seq, _ = x_sample.shape
    depth = w_in.shape[0]
    assert depth == 1 and dec_seq == 1 and d == D_MODEL
    n_prompt = batch * seq
    n = n_prompt + n_seq
    past_len = page_table.shape[1] * PAGE_SIZE

    x_all = jnp.concatenate([x_prompt.reshape(n_prompt, d), x_sample.reshape(n_seq, d)], axis=0)
    pos = jnp.concatenate([jnp.tile(jnp.arange(seq, dtype=jnp.int32), batch),
                           jnp.full((n_seq,), past_len, jnp.int32)])
    tabs = _rope_tables(pos)

    row = lambda g: g.reshape(1, -1).astype(F32)
    win_p = _prep_win(w_in[0])
    walpha_p = jnp.zeros((LANES, GLA_QK_W), F32).at[MLA_ROPE:MLA_ROPE + GLA_LOWRANK].set(gla_w_alpha[0]).astype(BF16)
    wuq_p, wuqs_p = _prep_wuq(mla_w_uq[0])

    gq, gk, gv, gr, gg, qm, ckv, kpe128 = _proj(
        x_all, tabs, row(norm_attn[0]), win_p, walpha_p, row(gla_b_alpha[0]), row(mla_q_norm[0]),
        wuq_p, wuqs_p, row(mla_kv_norm[0]))

    wuk2 = mla_w_uk[0].reshape(MLA_KV_RANK, MLA_HEADS * MLA_NOPE).astype(BF16)
    wuv2 = mla_w_uv[0].reshape(MLA_KV_RANK, MLA_V_W).astype(BF16)
    kfull, vfull = _kv_up(ckv, kpe128, wuk2, wuv2, n_prompt)
    mla_o_p = _mla_prompt(qm, kfull, vfull, batch, seq)

    wuk_h = jnp.transpose(mla_w_uk[0], (1, 0, 2)).astype(BF16)
    wuv_h = jnp.transpose(mla_w_uv[0], (1, 0, 2)).astype(BF16)
    qfull = jnp.transpose(_qlat(qm, wuk_h, n_prompt, n_seq), (1, 0, 2))
    olat = _decode(page_table, qfull, ckv[n_prompt:].reshape(n_seq, 1, MLA_KV_RANK),
                   kpe128[n_prompt:].reshape(n_seq, 1, LANES), cache_ckv, jnp.swapaxes(cache_kpe, 2, 3))
    mla_o_s = _o_up(jnp.transpose(olat, (1, 0, 2)), wuv_h)

    gla_o_p, gla_state_p = _gla_prompt(gq, gk, gv, gg, batch, seq)
    gla_state_s, gla_o_s = _gla_sample(gq[n_prompt:].T, gk[n_prompt:].T, gg[n_prompt:].T, gv[n_prompt:],
                                       state_gla[0])

    gla_o = jnp.concatenate([gla_o_p, gla_o_s], axis=0)
    mla_o = jnp.concatenate([mla_o_p, mla_o_s], axis=0)
    x2 = _out_proj(x_all, gla_o, gr, mla_o, row(gla_norm[0]), w_out[0].astype(BF16))

    keys2 = peer_keys[0].reshape(2 * PEER_HEADS, PEER_NKEYS, PEER_DKEY // 2).astype(BF16)
    xn, sa, sb, sg = _peer_query(x2, row(norm_ffn[0]), peer_w_q[0].astype(BF16), keys2)
    ut = jnp.transpose(peer_u[0].astype(BF16).reshape(-1, PEER_EXPERTS, d), (0, 2, 1))
    y = _peer_dense(xn, sa, sb, sg, ut, peer_v[0].astype(BF16), x2, row(norm_final))

    y_prompt = y[:n_prompt].reshape(batch, seq, d)
    y_sample = y[n_prompt:].reshape(n_seq, dec_seq, d)
    ckv_p = ckv[:n_prompt].reshape(1, batch, seq, MLA_KV_RANK)
    kpe_p = kpe128[:n_prompt, :MLA_ROPE].reshape(1, batch, seq, MLA_ROPE)
    ckv_s = ckv[n_prompt:].reshape(1, n_seq, dec_seq, MLA_KV_RANK)
    kpe_s = kpe128[n_prompt:, :MLA_ROPE].reshape(1, n_seq, dec_seq, MLA_ROPE)
    return (y_prompt, y_sample, ckv_p, kpe_p, gla_state_p[None], ckv_s, kpe_s, gla_state_s[None])
```

```python
import functools

import jax
import jax.numpy as jnp
from jax import lax
from jax.experimental import pallas as pl
from jax.experimental.pallas import tpu as pltpu

F32 = jnp.float32
BF16 = jnp.bfloat16

D_MODEL = 2048
PAGE_SIZE = 128
GLA_HEADS = 4
GLA_DK = 128
GLA_DV = 256
GLA_LOWRANK = 16
GLA_TAU = 16.0
MLA_HEADS = 8
MLA_Q_RANK = 512
MLA_KV_RANK = 512
MLA_NOPE = 128
MLA_ROPE = 64
MLA_DV = 128
MLA_SCALE = (MLA_NOPE + MLA_ROPE) ** -0.5
ROPE_THETA = 10000.0
PEER_HEADS = 8
PEER_NKEYS = 128
PEER_DKEY = 256
PEER_TOPK = 16
EPS = 1e-6

GLA_QK_W = GLA_HEADS * GLA_DK
GLA_V_W = GLA_HEADS * GLA_DV
MLA_V_W = MLA_HEADS * MLA_DV
MLA_QHEAD_PAD = 256
MLA_VAUG = 256
LANES = 128

VMEM_LIMIT_BYTES = 56 * 1024 * 1024

PROJ_TOKENS = 320
KV_TOKENS = 512
ATTN_BLOCK = 512
ATTN_HEADS = 4
GLA_TOKENS = 256
GLA_SUB = 16
DEC_PAGES = 32
GLA_S_DK_BLOCK = 32
OUT_TOKENS = 320
PEERQ_TOKENS = 128
PEERQ_SIDES = 4
PEERQ_HEADS = 2
PEER_TOKENS = 416
PEER_EXPERTS = 512
PEER_G_PAD = 8
PEER_G_UNROLL = 8


def _params(*sem):
    return pltpu.CompilerParams(dimension_semantics=sem, vmem_limit_bytes=VMEM_LIMIT_BYTES)


def _const_spec(shape):
    nd = len(shape)
    return pl.BlockSpec(shape, lambda *_: (0,) * nd, pipeline_mode=pl.Buffered(1))


def _rms(x, g):
    return x * lax.rsqrt(jnp.mean(x * x, axis=-1, keepdims=True) + EPS) * g


def _gelu(x):
    return 0.5 * x * (1.0 + lax.erf(x * (0.5 ** 0.5)))


def _dot(a, b):
    return jnp.dot(a, b, preferred_element_type=F32)


def _dot_nt(a, b):
    return lax.dot_general(a, b, (((1,), (1,)), ((), ())), preferred_element_type=F32)


def _dot_tn(a, b):
    return lax.dot_general(a, b, (((0,), (0,)), ((), ())), preferred_element_type=F32)


_C_GQ, _C_GK, _C_GV, _C_GR = 0, 512, 1024, 2048
_C_CQ, _C_CKV, _C_KPE, _C_KSW, _C_END = 3072, 3584, 4096, 4224, 4352


def _proj_kernel(x_ref, gattn_ref, win_ref, walpha_ref, balpha_ref, gqn_ref, wuq_ref, wuqs_ref, gkv_ref,
                 cq_ref, sq_ref, ck_ref, sk_ref,
                 oq_ref, ok_ref, ov_ref, or_ref, og_ref, oqm_ref, ockv_ref, okpe_ref):
    h = _rms(x_ref[...], gattn_ref[...]).astype(BF16)

    def col(a, b):
        return _dot(h, win_ref[:, a:b])

    oq_ref[...] = col(_C_GQ, _C_GK) * (GLA_DK ** -0.5)
    ok_ref[...] = col(_C_GK, _C_GV)
    ov_ref[...] = col(_C_GV, _C_GR)
    or_ref[...] = col(_C_GR, _C_CQ)
    grp = col(_C_KPE, _C_KSW)
    swp = col(_C_KSW, _C_END)
    xa = _dot(grp.astype(BF16), walpha_ref[...]) + balpha_ref[...]
    og_ref[...] = jax.nn.log_sigmoid(xa) / GLA_TAU
    okpe_ref[...] = grp * ck_ref[...] + swp * sk_ref[...]
    ockv_ref[...] = _rms(col(_C_CKV, _C_KPE), gkv_ref[...])
    cqn = _rms(col(_C_CQ, _C_CKV), gqn_ref[...]).astype(BF16)
    cq_tab = cq_ref[...]
    sq_tab = sq_ref[...]
    for hd in range(MLA_HEADS):
        a, b = hd * MLA_QHEAD_PAD, (hd + 1) * MLA_QHEAD_PAD
        raw = _dot(cqn, wuq_ref[:, a:b])
        sw = _dot(cqn, wuqs_ref[:, a:b])
        oqm_ref[:, a:b] = (raw * cq_tab + sw * sq_tab).astype(BF16)


def _proj(x_all, tabs, gattn, win_p, walpha_p, balpha, gqn, wuq_p, wuqs_p, gkv):
    n = x_all.shape[0]
    tb = PROJ_TOKENS
    cq_tab, sq_tab, ck_tab, sk_tab = tabs

    def rows(w):
        return pl.BlockSpec((tb, w), lambda i: (i, 0))

    out_w = (GLA_QK_W, GLA_QK_W, GLA_V_W, GLA_V_W, GLA_QK_W, MLA_HEADS * MLA_QHEAD_PAD, MLA_KV_RANK, LANES)
    out_dt = (F32, F32, F32, F32, F32, BF16, F32, F32)
    return pl.pallas_call(
        _proj_kernel,
        grid=(n // tb,),
        in_specs=[rows(D_MODEL), _const_spec(gattn.shape), _const_spec(win_p.shape), _const_spec(walpha_p.shape),
                  _const_spec(balpha.shape), _const_spec(gqn.shape), _const_spec(wuq_p.shape),
                  _const_spec(wuqs_p.shape), _const_spec(gkv.shape),
                  rows(MLA_QHEAD_PAD), rows(MLA_QHEAD_PAD), rows(LANES), rows(LANES)],
        out_specs=[rows(w) for w in out_w],
        out_shape=[jax.ShapeDtypeStruct((n, w), dt) for w, dt in zip(out_w, out_dt)],
        compiler_params=_params("parallel"),
    )(x_all, gattn, win_p, walpha_p, balpha, gqn, wuq_p, wuqs_p, gkv, cq_tab, sq_tab, ck_tab, sk_tab)


def _kv_kernel(ckv_ref, kpe_ref, wuk_ref, wuv_ref, k_ref, v_ref):
    c = ckv_ref[...].astype(BF16)
    kpe = kpe_ref[...].astype(BF16)
    kn = _dot(c, wuk_ref[...]).astype(BF16)
    vv = _dot(c, wuv_ref[...]).astype(BF16)
    ones = jnp.ones((vv.shape[0], MLA_DV), BF16)
    for hd in range(MLA_HEADS):
        a = hd * MLA_QHEAD_PAD
        k_ref[:, a:a + MLA_NOPE] = kn[:, hd * MLA_NOPE:(hd + 1) * MLA_NOPE]
        k_ref[:, a + MLA_NOPE:a + MLA_QHEAD_PAD] = kpe
        b = hd * MLA_VAUG
        v_ref[:, b:b + MLA_DV] = vv[:, hd * MLA_DV:(hd + 1) * MLA_DV]
        v_ref[:, b + MLA_DV:b + MLA_VAUG] = ones


def _kv_up(ckv, kpe128, wuk2, wuv2, n_prompt):
    tb = KV_TOKENS
    return pl.pallas_call(
        _kv_kernel,
        grid=(n_prompt // tb,),
        in_specs=[pl.BlockSpec((tb, MLA_KV_RANK), lambda i: (i, 0)), pl.BlockSpec((tb, LANES), lambda i: (i, 0)),
                  _const_spec(wuk2.shape), _const_spec(wuv2.shape)],
        out_specs=[pl.BlockSpec((tb, MLA_HEADS * MLA_QHEAD_PAD), lambda i: (i, 0)),
                   pl.BlockSpec((tb, MLA_HEADS * MLA_VAUG), lambda i: (i, 0))],
        out_shape=[jax.ShapeDtypeStruct((n_prompt, MLA_HEADS * MLA_QHEAD_PAD), BF16),
                   jax.ShapeDtypeStruct((n_prompt, MLA_HEADS * MLA_VAUG), BF16)],
        compiler_params=_params("parallel"),
    )(ckv, kpe128, wuk2, wuv2)


def _mla_prompt_kernel(q_ref, k_ref, v_ref, o_ref, m_scr, acc_scr):
    blk = ATTN_BLOCK
    i = pl.program_id(2)
    m_scr[...] = jnp.full(m_scr.shape, -jnp.inf, F32)
    acc_scr[...] = jnp.zeros(acc_scr.shape, F32)

    def step(kb, masked):
        r0 = pl.multiple_of(kb * blk, blk)

        def scores(g):
            qa, qb = g * MLA_QHEAD_PAD, (g + 1) * MLA_QHEAD_PAD
            s = _dot_nt(q_ref[:, qa:qb], k_ref[pl.ds(r0, blk), qa:qb]) * MLA_SCALE
            if masked:
                row = lax.broadcasted_iota(jnp.int32, (blk, blk), 0)
                colm = lax.broadcasted_iota(jnp.int32, (blk, blk), 1)
                s = jnp.where(row >= colm, s, -jnp.inf)
            return s

        def softmax_pv(g, s):
            m_old = m_scr[g]
            m_new = jnp.maximum(m_old, jnp.max(s, axis=-1, keepdims=True))
            alpha = jnp.exp(m_old - m_new)
            p = jnp.exp(s - m_new)
            acc_scr[g] = alpha * acc_scr[g] + _dot(p.astype(BF16),
                                                   v_ref[pl.ds(r0, blk), g * MLA_VAUG:(g + 1) * MLA_VAUG])
            m_scr[g] = m_new

        s_next = scores(0)
        for g in range(ATTN_HEADS):
            s_cur = s_next
            if g + 1 < ATTN_HEADS:
                s_next = scores(g + 1)
            softmax_pv(g, s_cur)

    def body(kb, carry):
        step(kb, False)
        return carry

    lax.fori_loop(0, i, body, 0)
    step(i, True)
    for g in range(ATTN_HEADS):
        acc = acc_scr[g]
        o_ref[:, g * MLA_DV:(g + 1) * MLA_DV] = acc[:, :MLA_DV] / acc[:, MLA_DV:MLA_DV + 1]


def _mla_prompt(qm, kfull, v, batch, seq):
    blk = ATTN_BLOCK
    nq = seq // blk
    hg = ATTN_HEADS
    return pl.pallas_call(
        _mla_prompt_kernel,
        grid=(batch, MLA_HEADS // hg, nq),
        in_specs=[pl.BlockSpec((blk, hg * MLA_QHEAD_PAD), lambda b, h, i: (b * nq + i, h)),
                  pl.BlockSpec((seq, hg * MLA_QHEAD_PAD), lambda b, h, i: (b, h)),
                  pl.BlockSpec((seq, hg * MLA_VAUG), lambda b, h, i: (b, h))],
        out_specs=pl.BlockSpec((blk, hg * MLA_DV), lambda b, h, i: (b * nq + i, h)),
        out_shape=jax.ShapeDtypeStruct((batch * seq, MLA_V_W), F32),
        scratch_shapes=[pltpu.VMEM((hg, blk, 1), F32), pltpu.VMEM((hg, blk, MLA_VAUG), F32)],
        compiler_params=_params("parallel", "parallel", "arbitrary"),
    )(qm, kfull, v)


def _qlat_kernel(q_ref, wuk_ref, o_ref):
    q = q_ref[...]
    qlat = _dot_nt(q[:, :MLA_NOPE], wuk_ref[0])
    o_ref[0] = jnp.concatenate([qlat, q[:, MLA_NOPE:MLA_NOPE + MLA_ROPE].astype(F32)], axis=-1).astype(BF16)


def _qlat(qm, wuk_h, n_prompt, n_sample):
    rb = n_prompt // n_sample
    width = MLA_KV_RANK + MLA_ROPE
    return pl.pallas_call(
        _qlat_kernel,
        grid=(MLA_HEADS,),
        in_specs=[pl.BlockSpec((n_sample, MLA_QHEAD_PAD), lambda h: (rb, h)),
                  pl.BlockSpec((1, MLA_KV_RANK, MLA_NOPE), lambda h: (h, 0, 0))],
        out_specs=pl.BlockSpec((1, n_sample, width), lambda h: (h, 0, 0)),
        out_shape=jax.ShapeDtypeStruct((MLA_HEADS, n_sample, width), BF16),
        compiler_params=_params("parallel"),
    )(qm, wuk_h)


def _decode_kernel(pt_ref, q_ref, cnew_ref, knew_ref, ckv_hbm, kpe_hbm, o_ref,
                   cbuf, kbuf, sem, m_scr, l_scr, acc_scr):
    np_ = DEC_PAGES
    s_id = pl.program_id(0)
    c = pl.program_id(1)
    n_chunks = pl.num_programs(1)
    n_steps = pl.num_programs(0) * n_chunks
    step = s_id * n_chunks + c
    slot = step % 2
    nxt = jnp.minimum(step + 1, n_steps - 1)
    nxt_seq = nxt // n_chunks
    nxt_page0 = (nxt % n_chunks) * np_
    rank = MLA_KV_RANK

    def page_copies(seq, page0, pg, slot_):
        page = pt_ref[seq, page0 + pg]
        return (pltpu.make_async_copy(ckv_hbm.at[0, page], cbuf.at[slot_, pg], sem.at[0, slot_]),
                pltpu.make_async_copy(kpe_hbm.at[0, page], kbuf.at[slot_, pg], sem.at[1, slot_]))

    def wait_slot(slot_):
        pltpu.make_async_copy(ckv_hbm.at[0, pl.ds(0, np_)], cbuf.at[slot_], sem.at[0, slot_]).wait()
        pltpu.make_async_copy(kpe_hbm.at[0, pl.ds(0, np_)], kbuf.at[slot_], sem.at[1, slot_]).wait()

    @pl.when(step == 0)
    def _():
        for pg in range(np_):
            for cp in page_copies(0, 0, pg, 0):
                cp.start()

    @pl.when(c == 0)
    def _():
        m_scr[...] = jnp.full(m_scr.shape, -jnp.inf, F32)
        l_scr[...] = jnp.zeros(l_scr.shape, F32)
        acc_scr[...] = jnp.zeros(acc_scr.shape, F32)

    wait_slot(slot)
    q = q_ref[0]
    q_lat, q_pe = q[:, :rank], q[:, rank:]
    s_parts = []
    for pg in range(np_):
        for cp in page_copies(nxt_seq, nxt_page0, pg, 1 - slot):
            cp.start()
        s_parts.append(_dot_nt(q_lat, cbuf[slot, pg].astype(BF16)) + _dot(q_pe, kbuf[slot, pg].astype(BF16)))
    s = jnp.concatenate(s_parts, axis=-1) * MLA_SCALE
    m_old = m_scr[...]
    m_new = jnp.maximum(m_old, jnp.max(s, axis=-1, keepdims=True))
    alpha = jnp.exp(m_old - m_new)
    p = jnp.exp(s - m_new)
    l_scr[...] = alpha * l_scr[...] + jnp.sum(p, axis=-1, keepdims=True)
    pb = p.astype(BF16)
    pv = _dot(pb[:, :PAGE_SIZE], cbuf[slot, 0].astype(BF16))
    for pg in range(1, np_):
        pv = pv + _dot(pb[:, pg * PAGE_SIZE:(pg + 1) * PAGE_SIZE], cbuf[slot, pg].astype(BF16))
    acc_scr[...] = alpha * acc_scr[...] + pv
    m_scr[...] = m_new

    @pl.when(step == n_steps - 1)
    def _():
        wait_slot(1 - slot)

    @pl.when(c == pl.num_programs(1) - 1)
    def _():
        cn = cnew_ref[0].astype(BF16).astype(F32)
        kn = knew_ref[0][:, :MLA_ROPE].astype(BF16).astype(F32)
        qf = q.astype(F32)
        s_new = (jnp.sum(qf[:, :rank] * cn, axis=-1, keepdims=True)
                 + jnp.sum(qf[:, rank:] * kn, axis=-1, keepdims=True)) * MLA_SCALE
        m_old2 = m_scr[...]
        m_fin = jnp.maximum(m_old2, s_new)
        a2 = jnp.exp(m_old2 - m_fin)
        p_new = jnp.exp(s_new - m_fin)
        l_fin = a2 * l_scr[...] + p_new
        acc = a2 * acc_scr[...] + p_new.astype(BF16).astype(F32) * cn
        o_ref[0] = acc / l_fin


def _decode(page_table, qfull, ckv_new, kpe_new, cache_ckv, cache_kpe_t):
    n_seq, n_pages = page_table.shape
    np_ = DEC_PAGES
    width = MLA_KV_RANK + MLA_ROPE

    grid_spec = pltpu.PrefetchScalarGridSpec(
        num_scalar_prefetch=1,
        grid=(n_seq, n_pages // np_),
        in_specs=[pl.BlockSpec((1, MLA_HEADS, width), lambda s, c, pt: (s, 0, 0)),
                  pl.BlockSpec((1, 1, MLA_KV_RANK), lambda s, c, pt: (s, 0, 0)),
                  pl.BlockSpec((1, 1, LANES), lambda s, c, pt: (s, 0, 0)),
                  pl.BlockSpec(memory_space=pl.ANY), pl.BlockSpec(memory_space=pl.ANY)],
        out_specs=pl.BlockSpec((1, MLA_HEADS, MLA_KV_RANK), lambda s, c, pt: (s, 0, 0)),
        scratch_shapes=[pltpu.VMEM((2, np_, PAGE_SIZE, MLA_KV_RANK), F32),
                        pltpu.VMEM((2, np_, MLA_ROPE, PAGE_SIZE), F32),
                        pltpu.SemaphoreType.DMA((2, 2)),
                        pltpu.VMEM((MLA_HEADS, 1), F32), pltpu.VMEM((MLA_HEADS, 1), F32),
                        pltpu.VMEM((MLA_HEADS, MLA_KV_RANK), F32)],
    )
    return pl.pallas_call(
        _decode_kernel,
        grid_spec=grid_spec,
        out_shape=jax.ShapeDtypeStruct((n_seq, MLA_HEADS, MLA_KV_RANK), F32),
        compiler_params=_params("arbitrary", "arbitrary"),
    )(page_table, qfull, ckv_new, kpe_new, cache_ckv, cache_kpe_t)


def _oup_kernel(o_ref, wuv_ref, out_ref):
    out_ref[...] = _dot(o_ref[0].astype(BF16), wuv_ref[0])


def _o_up(olat_h, wuv_h):
    n_sample = olat_h.shape[1]
    return pl.pallas_call(
        _oup_kernel,
        grid=(MLA_HEADS,),
        in_specs=[pl.BlockSpec((1, n_sample, MLA_KV_RANK), lambda h: (h, 0, 0)),
                  pl.BlockSpec((1, MLA_KV_RANK, MLA_DV), lambda h: (h, 0, 0))],
        out_specs=pl.BlockSpec((n_sample, MLA_DV), lambda h: (0, h)),
        out_shape=jax.ShapeDtypeStruct((n_sample, MLA_V_W), F32),
        compiler_params=_params("parallel"),
    )(olat_h, wuv_h)


def _gla_prompt_kernel(q_ref, k_ref, v_ref, g_ref, o_ref, sout_ref, st_scr):
    sub = GLA_SUB
    c = pl.program_id(1)

    @pl.when(c == 0)
    def _():
        st_scr[...] = jnp.zeros(st_scr.shape, F32)

    row = lax.broadcasted_iota(jnp.int32, (sub, sub), 0)
    colm = lax.broadcasted_iota(jnp.int32, (sub, sub), 1)
    tri = (row >= colm).astype(F32)
    rowi = lax.broadcasted_iota(jnp.int32, (sub, 1), 0)

    def chunk(ci, carry):
        r0 = pl.multiple_of(ci * sub, sub)
        b_all = jnp.dot(tri, g_ref[pl.ds(r0, sub), :], precision=lax.Precision.HIGHEST,
                        preferred_element_type=F32)
        for hd in range(GLA_HEADS):
            ka, kb = hd * GLA_DK, (hd + 1) * GLA_DK
            va, vb = hd * GLA_DV, (hd + 1) * GLA_DV
            q = q_ref[pl.ds(r0, sub), ka:kb]
            k = k_ref[pl.ds(r0, sub), ka:kb]
            v = v_ref[pl.ds(r0, sub), va:vb]
            b = b_all[:, ka:kb]
            st = st_scr[hd]
            o = _dot_nt((q * jnp.exp(b)).astype(BF16), st.astype(BF16))
            for j in range(sub):
                w = q * k[j:j + 1, :] * jnp.exp(jnp.minimum(b - b[j:j + 1, :], 0.0))
                a = jnp.sum(w, axis=-1, keepdims=True)
                o = o + jnp.where(rowi >= j, a, 0.0) * v[j:j + 1, :]
            o_ref[pl.ds(r0, sub), va:vb] = o
            bl = b[sub - 1:sub, :]
            kt = (k * jnp.exp(bl - b)).astype(BF16)
            st_scr[hd] = st * jnp.exp(bl) + _dot_tn(v.astype(BF16), kt)
        return carry

    lax.fori_loop(0, GLA_TOKENS // sub, chunk, 0)

    @pl.when(c == pl.num_programs(1) - 1)
    def _():
        for hd in range(GLA_HEADS):
            sout_ref[0, hd] = st_scr[hd].T


def _gla_prompt(gq, gk, gv, gg, batch, seq):
    tc = GLA_TOKENS
    nc = seq // tc

    def rows(w):
        return pl.BlockSpec((tc, w), lambda b, c: (b * nc + c, 0))

    return pl.pallas_call(
        _gla_prompt_kernel,
        grid=(batch, nc),
        in_specs=[rows(GLA_QK_W), rows(GLA_QK_W), rows(GLA_V_W), rows(GLA_QK_W)],
        out_specs=[rows(GLA_V_W), pl.BlockSpec((1, GLA_HEADS, GLA_DK, GLA_DV), lambda b, c: (b, 0, 0, 0))],
        out_shape=[jax.ShapeDtypeStruct((batch * seq, GLA_V_W), F32),
                   jax.ShapeDtypeStruct((batch, GLA_HEADS, GLA_DK, GLA_DV), F32)],
        scratch_shapes=[pltpu.VMEM((GLA_HEADS, GLA_DV, GLA_DK), F32)],
        compiler_params=_params("parallel", "arbitrary"),
    )(gq, gk, gv, gg)


def _gla_sample_kernel(qt_ref, kt_ref, gt_ref, v_ref, s_ref, snew_ref, o_ref):
    c = pl.program_id(1)
    n_seq = s_ref.shape[0]

    @pl.when(c == 0)
    def _():
        o_ref[...] = jnp.zeros(o_ref.shape, F32)

    qt = qt_ref[...]
    kt = kt_ref[...]
    at = jnp.exp(gt_ref[...])
    for s in range(n_seq):
        sn = at[:, s:s + 1] * s_ref[s, 0] + kt[:, s:s + 1] * v_ref[s:s + 1, :]
        snew_ref[s, 0] = sn
        o_ref[s:s + 1, :] += jnp.sum(qt[:, s:s + 1] * sn, axis=0, keepdims=True)


def _gla_sample(qt, kt, gt, v, state):
    n_seq = state.shape[0]
    rb = GLA_S_DK_BLOCK
    nb = GLA_DK // rb
    vec = pl.BlockSpec((rb, n_seq), lambda h, c: (h * nb + c, 0))
    st = pl.BlockSpec((n_seq, 1, rb, GLA_DV), lambda h, c: (0, h, c, 0))
    return pl.pallas_call(
        _gla_sample_kernel,
        grid=(GLA_HEADS, nb),
        in_specs=[vec, vec, vec, pl.BlockSpec((n_seq, GLA_DV), lambda h, c: (0, h)), st],
        out_specs=[st, pl.BlockSpec((n_seq, GLA_DV), lambda h, c: (0, h))],
        out_shape=[jax.ShapeDtypeStruct(state.shape, F32), jax.ShapeDtypeStruct((n_seq, GLA_V_W), F32)],
        compiler_params=_params("parallel", "arbitrary"),
    )(qt, kt, gt, v, state)


def _out_kernel(x_ref, go_ref, gr_ref, mo_ref, gn_ref, wout_ref, o_ref):
    gn = gn_ref[...]
    gr = gr_ref[...]
    acc = x_ref[...] + _dot(mo_ref[...].astype(BF16), wout_ref[GLA_V_W:, :])
    parts = []
    for hd in range(GLA_HEADS):
        a, b = hd * GLA_DV, (hd + 1) * GLA_DV
        parts.append(_rms(go_ref[:, a:b], gn) * jax.nn.silu(gr[:, a:b]))
    go = jnp.concatenate(parts, axis=-1).astype(BF16)
    o_ref[...] = acc + _dot(go, wout_ref[:GLA_V_W, :])


def _out_proj(x_all, gla_o, gr, mla_o, gn, wout):
    n = x_all.shape[0]
    tb = OUT_TOKENS

    def rows(w):
        return pl.BlockSpec((tb, w), lambda i: (i, 0))

    return pl.pallas_call(
        _out_kernel,
        grid=(n // tb,),
        in_specs=[rows(D_MODEL), rows(GLA_V_W), rows(GLA_V_W), rows(MLA_V_W), _const_spec(gn.shape),
                  _const_spec(wout.shape)],
        out_specs=rows(D_MODEL),
        out_shape=jax.ShapeDtypeStruct((n, D_MODEL), F32),
        compiler_params=_params("parallel"),
    )(x_all, gla_o, gr, mla_o, gn, wout)


def _topk_rows_multi(xs, ids, k):
    xs = list(xs)
    vals = [[] for _ in xs]
    poss = [[] for _ in xs]
    for _ in range(k):
        for n, x in enumerate(xs):
            m = jnp.max(x, axis=0, keepdims=True)
            p = jnp.min(jnp.where(x == m, ids, jnp.inf), axis=0, keepdims=True)
            xs[n] = jnp.where(ids == p, -jnp.inf, x)
            vals[n].append(m)
            poss[n].append(p)
    return [(jnp.concatenate(v, axis=0), jnp.concatenate(p, axis=0)) for v, p in zip(vals, poss)]


def _topk_rows(x, ids, k):
    return _topk_rows_multi([x], ids, k)[0]


def _row_ids(n_rows, n_lanes, start, step):
    return (lax.broadcasted_iota(jnp.int32, (n_rows, n_lanes), 0) * step + start).astype(F32)


def _pair_candidates(s1, s2):
    k = PEER_TOPK
    n = s1.shape[1]
    lo = lax.broadcasted_iota(jnp.int32, (k, n), 0) < 4
    vals = [s1[0:1, :] + s2]
    ids = [_row_ids(k, n, 0, 1)]
    for r in (1, 2, 3):
        vals.append(s1[r:r + 1, :] + s2[0:8, :])
        ids.append(_row_ids(8, n, r * k, 1))
    vals.append(jnp.where(lo, -jnp.inf, s1 + s2[0:1, :]))
    ids.append(_row_ids(k, n, 0, k))
    for c in (1, 2):
        vals.append(jnp.where(lo[0:8], -jnp.inf, s1[0:8, :] + s2[c:c + 1, :]))
        ids.append(_row_ids(8, n, c, k))
    return jnp.concatenate(vals, axis=0), jnp.concatenate(ids, axis=0)


def _peerq_kernel(x_ref, gffn_ref, wq_ref, keys_ref, xn_ref, a_ref, b_ref, g_ref,
                  q_scr, v_scr, i_scr, sa_scr, sb_scr, sg_scr):
    topk = PEER_TOPK
    half = PEER_DKEY // 2
    xn = _rms(x_ref[...], gffn_ref[...]).astype(BF16)
    xn_ref[...] = xn
    q = _dot(xn, wq_ref[...])
    for hp in range(2 * PEER_HEADS):
        q_scr[hp] = q[:, hp * half:(hp + 1) * half].astype(BF16)

    key_ids = _row_ids(PEER_NKEYS, LANES, 0, 1)

    def side(it, carry):
        hps = [it * PEERQ_SIDES + u for u in range(PEERQ_SIDES)]
        scores = [_dot_nt(keys_ref[hp], q_scr[hp]) for hp in hps]
        for hp, (vals, poss) in zip(hps, _topk_rows_multi(scores, key_ids, topk)):
            v_scr[hp] = vals
            i_scr[hp] = poss
        return carry

    lax.fori_loop(0, 2 * PEER_HEADS // PEERQ_SIDES, side, 0)

    r16 = _row_ids(topk, LANES, 0, 1)

    def head(it, carry):
        hds = [it * PEERQ_HEADS + u for u in range(PEERQ_HEADS)]
        cands = [_pair_candidates(v_scr[2 * hd], v_scr[2 * hd + 1]) for hd in hds]
        tops = _topk_rows_multi([cv for cv, _ in cands], cands[0][1], topk)
        for hd, (top_s, pos) in zip(hds, tops):
            i1, i2 = i_scr[2 * hd], i_scr[2 * hd + 1]
            e = jnp.exp(top_s - top_s[0:1, :])
            gate = e / jnp.sum(e, axis=0, keepdims=True)
            r = jnp.floor(pos * (1.0 / topk))
            cc = pos - r * topk
            a_rows, b_rows = [], []
            for kk in range(topk):
                a_rows.append(jnp.sum(jnp.where(r16 == r[kk:kk + 1, :], i1, 0.0), axis=0, keepdims=True))
                b_rows.append(jnp.sum(jnp.where(r16 == cc[kk:kk + 1, :], i2, 0.0), axis=0, keepdims=True))
            base = pl.multiple_of(hd * topk, topk)
            sa_scr[pl.ds(base, topk), :] = jnp.concatenate(a_rows, axis=0)
            sb_scr[pl.ds(base, topk), :] = jnp.concatenate(b_rows, axis=0)
            sg_scr[pl.ds(base, topk), :] = gate
        return carry

    lax.fori_loop(0, PEER_HEADS // PEERQ_HEADS, head, 0)
    a_ref[...] = sa_scr[...].T
    b_ref[...] = sb_scr[...].T
    g_ref[...] = sg_scr[...].T


def _peer_query(x2, gffn, wq, keys2):
    n = x2.shape[0]
    tb = PEERQ_TOKENS
    nslot = PEER_HEADS * PEER_TOPK
    half = PEER_DKEY // 2

    def rows(w):
        return pl.BlockSpec((tb, w), lambda i: (i, 0))

    return pl.pallas_call(
        _peerq_kernel,
        grid=(n // tb,),
        in_specs=[rows(D_MODEL), _const_spec(gffn.shape), _const_spec(wq.shape), _const_spec(keys2.shape)],
        out_specs=[rows(D_MODEL), rows(nslot), rows(nslot), rows(nslot)],
        out_shape=[jax.ShapeDtypeStruct((n, D_MODEL), BF16)] + [jax.ShapeDtypeStruct((n, nslot), F32)] * 3,
        scratch_shapes=[pltpu.VMEM((2 * PEER_HEADS, tb, half), BF16),
                        pltpu.VMEM((2 * PEER_HEADS, PEER_TOPK, tb), F32),
                        pltpu.VMEM((2 * PEER_HEADS, PEER_TOPK, tb), F32),
                        pltpu.VMEM((nslot, tb), F32), pltpu.VMEM((nslot, tb), F32), pltpu.VMEM((nslot, tb), F32)],
        compiler_params=_params("parallel"),
    )(x2, gffn, wq, keys2)


def _peer_dense_kernel(xn_ref, a_ref, b_ref, g_ref, ut_ref, v_ref, x2_ref, gfin_ref, out_ref, gp_scr):
    tb = PEER_TOKENS
    te = PEER_EXPERTS
    stride = tb + PEER_G_PAD
    j = pl.program_id(1)

    @pl.when(j == 0)
    def _():
        out_ref[...] = jnp.zeros(out_ref.shape, F32)
        sub_iota = lax.broadcasted_iota(jnp.int32, (PEER_NKEYS, LANES), 0).astype(F32)

        def tok(t8, carry):
            for u in range(PEER_G_UNROLL):
                t = t8 * PEER_G_UNROLL + u
                arow = a_ref[pl.ds(t, 1), :]
                brow = b_ref[pl.ds(t, 1), :]
                grow = g_ref[pl.ds(t, 1), :]
                at = jnp.where(sub_iota == arow, 1.0, 0.0).astype(BF16)
                bt = jnp.where(sub_iota == brow, grow, 0.0).astype(BF16)
                gp_scr[pl.ds(t, PEER_NKEYS, stride=stride), :] = _dot_nt(at, bt)
            return carry

        lax.fori_loop(0, tb // PEER_G_UNROLL, tok, 0)

    act = _gelu(_dot(xn_ref[...], ut_ref[0]))
    n_a = te // LANES
    parts = []
    for jj in range(n_a):
        r0 = pl.multiple_of((j * n_a + jj) * stride, 8)
        parts.append(gp_scr[pl.ds(r0, tb), :] * act[:, jj * LANES:(jj + 1) * LANES])
    p = jnp.concatenate(parts, axis=-1).astype(BF16)
    out_ref[...] += _dot(p, v_ref[...])

    @pl.when(j == pl.num_programs(1) - 1)
    def _():
        out_ref[...] = _rms(x2_ref[...] + out_ref[...], gfin_ref[...])


def _peer_dense(xn, sa, sb, sg, ut, vv, x2, gfin):
    n = xn.shape[0]
    tb = PEER_TOKENS
    te = PEER_EXPERTS
    n_exp = vv.shape[0]
    nslot = PEER_HEADS * PEER_TOPK

    def rows(w, **kw):
        return pl.BlockSpec((tb, w), lambda i, j: (i, 0), **kw)

    once = dict(pipeline_mode=pl.Buffered(1))
    return pl.pallas_call(
        _peer_dense_kernel,
        grid=(n // tb, n_exp // te),
        in_specs=[rows(D_MODEL, **once), rows(nslot, **once), rows(nslot, **once), rows(nslot, **once),
                  pl.BlockSpec((1, D_MODEL, te), lambda i, j: (j, 0, 0)),
                  pl.BlockSpec((te, D_MODEL), lambda i, j: (j, 0)),
                  rows(D_MODEL, **once), _const_spec(gfin.shape)],
        out_specs=rows(D_MODEL),
        out_shape=jax.ShapeDtypeStruct((n, D_MODEL), F32),
        scratch_shapes=[pltpu.VMEM((PEER_NKEYS * (tb + PEER_G_PAD), LANES), F32)],
        compiler_params=_params("parallel", "arbitrary"),
    )(xn, sa, sb, sg, ut, vv, x2, gfin)


def _rope_tables(pos):
    half = MLA_ROPE // 2
    inv = ROPE_THETA ** (-jnp.arange(half, dtype=F32) / half)
    ang = pos.astype(F32)[:, None] * inv[None, :]
    cos, sin = jnp.cos(ang), jnp.sin(ang)
    n = pos.shape[0]
    cc = jnp.concatenate([cos, cos], axis=-1)
    ss = jnp.concatenate([-sin, sin], axis=-1)
    z64 = jnp.zeros((n, LANES - MLA_ROPE), F32)
    cq_tab = jnp.concatenate([jnp.ones((n, MLA_NOPE), F32), cc, z64], axis=-1)
    sq_tab = jnp.concatenate([jnp.zeros((n, MLA_NOPE), F32), ss, z64], axis=-1)
    ck_tab = jnp.concatenate([cc, z64], axis=-1)
    sk_tab = jnp.concatenate([ss, z64], axis=-1)
    return cq_tab, sq_tab, ck_tab, sk_tab


def _prep_win(w_in):
    d = w_in.shape[0]
    o = 0
    pieces = {}
    for name, width in (("gq", GLA_QK_W), ("gk", GLA_QK_W), ("gv", GLA_V_W), ("gr", GLA_V_W), ("ga", GLA_LOWRANK),
                        ("cq", MLA_Q_RANK), ("ckv", MLA_KV_RANK), ("kpe", MLA_ROPE)):
        pieces[name] = w_in[:, o:o + width]
        o += width
    half = MLA_ROPE // 2
    kpe = pieces["kpe"]
    ksw = jnp.concatenate([kpe[:, half:], kpe[:, :half]], axis=-1)
    z = lambda w: jnp.zeros((d, w), w_in.dtype)
    cols = [pieces["gq"], pieces["gk"], pieces["gv"], pieces["gr"], pieces["cq"], pieces["ckv"],
            kpe, pieces["ga"], z(LANES - MLA_ROPE - GLA_LOWRANK), ksw, z(LANES - MLA_ROPE)]
    return jnp.concatenate(cols, axis=-1).astype(BF16)


def _prep_wuq(w_uq):
    r = w_uq.shape[0]
    w = w_uq.reshape(r, MLA_HEADS, MLA_NOPE + MLA_ROPE)
    half = MLA_ROPE // 2
    nope, pe = w[..., :MLA_NOPE], w[..., MLA_NOPE:]
    pad = jnp.zeros((r, MLA_HEADS, MLA_QHEAD_PAD - MLA_NOPE - MLA_ROPE), w_uq.dtype)
    plain = jnp.concatenate([nope, pe, pad], axis=-1)
    swapped = jnp.concatenate([jnp.zeros_like(nope), pe[..., half:], pe[..., :half], pad], axis=-1)
    width = MLA_HEADS * MLA_QHEAD_PAD
    return plain.reshape(r, width).astype(BF16), swapped.reshape(r, width).astype(BF16)


def kernel(x_prompt, x_sample, cache_ckv, cache_kpe, state_gla, page_table, norm_attn, w_in, gla_w_alpha, gla_b_alpha, gla_norm, mla_q_norm, mla_w_uq, mla_kv_norm, mla_w_uk, mla_w_uv, w_out, norm_ffn, peer_w_q, peer_keys, peer_u, peer_v, norm_final):
    batch, seq, d = x_prompt.shape
    n_seq, dec_seq, _ = x_sample.shape
    depth = w_in.shape[0]
    assert depth == 1 and dec_seq == 1 and d == D_MODEL
    n_prompt = batch * seq
    n = n_prompt + n_seq
    past_len = page_table.shape[1] * PAGE_SIZE

    x_all = jnp.concatenate([x_prompt.reshape(n_prompt, d), x_sample.reshape(n_seq, d)], axis=0)
    pos = jnp.concatenate([jnp.tile(jnp.arange(seq, dtype=jnp.int32), batch),
                           jnp.full((n_seq,), past_len, jnp.int32)])
    tabs = _rope_tables(pos)

    row = lambda g: g.reshape(1, -1).astype(F32)
    win_p = _prep_win(w_in[0])
    walpha_p = jnp.zeros((LANES, GLA_QK_W), F32).at[MLA_ROPE:MLA_ROPE + GLA_LOWRANK].set(gla_w_alpha[0]).astype(BF16)
    wuq_p, wuqs_p = _prep_wuq(mla_w_uq[0])

    gq, gk, gv, gr, gg, qm, ckv, kpe128 = _proj(
        x_all, tabs, row(norm_attn[0]), win_p, walpha_p, row(gla_b_alpha[0]), row(mla_q_norm[0]),
        wuq_p, wuqs_p, row(mla_kv_norm[0]))

    wuk2 = mla_w_uk[0].reshape(MLA_KV_RANK, MLA_HEADS * MLA_NOPE).astype(BF16)
    wuv2 = mla_w_uv[0].reshape(MLA_KV_RANK, MLA_V_W).astype(BF16)
    kfull, vfull = _kv_up(ckv, kpe128, wuk2, wuv2, n_prompt)
    mla_o_p = _mla_prompt(qm, kfull, vfull, batch, seq)

    wuk_h = jnp.transpose(mla_w_uk[0], (1, 0, 2)).astype(BF16)
    wuv_h = jnp.transpose(mla_w_uv[0], (1, 0, 2)).astype(BF16)
    qfull = jnp.transpose(_qlat(qm, wuk_h, n_prompt, n_seq), (1, 0, 2))
    olat = _decode(page_table, qfull, ckv[n_prompt:].reshape(n_seq, 1, MLA_KV_RANK),
                   kpe128[n_prompt:].reshape(n_seq, 1, LANES), cache_ckv, jnp.swapaxes(cache_kpe, 2, 3))
    mla_o_s = _o_up(jnp.transpose(olat, (1, 0, 2)), wuv_h)

    gla_o_p, gla_state_p = _gla_prompt(gq, gk, gv, gg, batch, seq)
    gla_state_s, gla_o_s = _gla_sample(gq[n_prompt:].T, gk[n_prompt:].T, gg[n_prompt:].T, gv[n_prompt:],
                                       state_gla[0])

    gla_o = jnp.concatenate([gla_o_p, gla_o_s], axis=0)
    mla_o = jnp.concatenate([mla_o_p, mla_o_s], axis=0)
    x2 = _out_proj(x_all, gla_o, gr, mla_o, row(gla_norm[0]), w_out[0].astype(BF16))

    keys2 = peer_keys[0].reshape(2 * PEER_HEADS, PEER_NKEYS, PEER_DKEY // 2).astype(BF16)
    xn, sa, sb, sg = _peer_query(x2, row(norm_ffn[0]), peer_w_q[0].astype(BF16), keys2)
    ut = jnp.transpose(peer_u[0].astype(BF16).reshape(-1, PEER_EXPERTS, d), (0, 2, 1))
    y = _peer_dense(xn, sa, sb, sg, ut, peer_v[0].astype(BF16), x2, row(norm_final))

    y_prompt = y[:n_prompt].reshape(batch, seq, d)
    y_sample = y[n_prompt:].reshape(n_seq, dec_seq, d)
    ckv_p = ckv[:n_prompt].reshape(1, batch, seq, MLA_KV_RANK)
    kpe_p = kpe128[:n_prompt, :MLA_ROPE].reshape(1, batch, seq, MLA_ROPE)
    ckv_s = ckv[n_prompt:].reshape(1, n_seq, dec_seq, MLA_KV_RANK)
    kpe_s = kpe128[n_prompt:, :MLA_ROPE].reshape(1, n_seq, dec_seq, MLA_ROPE)
    return (y_prompt, y_sample, ckv_p, kpe_p, gla_state_p[None], ckv_s, kpe_s, gla_state_s[None])
```
